```python
import math
import jax, jax.numpy as jnp
from jax import lax
import numpy as np

D_MODEL = 1024
BATCH = 8
SEQ = 2048
DEPTH = 1

N_META = 16
C_CONV = D_MODEL
CONV_WIDTH = 31
N_HEADS = 16
HEAD_DIM = 64
ATTN_W = N_HEADS * HEAD_DIM
Q_BLOCK = 128
D_FF = ((8 * D_MODEL // 3 + 255) // 256) * 256
N_BRANCH = 2
IN_W = 2 * C_CONV + 3 * ATTN_W + N_BRANCH * D_MODEL
RMS_EPS = 1e-6
LN_EPS = 1e-5

kernel_name = "hybrid_conformer_stickbreaking_block"


def rms_norm(x, g):
    xf = x.astype(jnp.float32)
    y = xf * lax.rsqrt(jnp.mean(xf * xf, axis=-1, keepdims=True) + RMS_EPS)
    return (y * g.astype(jnp.float32)).astype(x.dtype)


def layer_norm(x, g, b):
    xf = x.astype(jnp.float32)
    mu = jnp.mean(xf, axis=-1, keepdims=True)
    xc = xf - mu
    var = jnp.mean(xc * xc, axis=-1, keepdims=True)
    y = xc * lax.rsqrt(var + LN_EPS) * g.astype(jnp.float32) + b.astype(jnp.float32)
    return y.astype(x.dtype)


def conformer_conv(u_glu, dw_w, dw_b, ln_g, ln_b, w_out):
    a, gate = jnp.split(u_glu, 2, axis=-1)
    u = a * jax.nn.sigmoid(gate)
    y = lax.conv_general_dilated(
        u, dw_w[:, None, :], window_strides=(1,),
        padding=[(CONV_WIDTH - 1, 0)],
        dimension_numbers=("NWC", "WIO", "NWC"),
        feature_group_count=C_CONV) + dw_b
    y = jax.nn.silu(layer_norm(y, ln_g, ln_b))
    return y @ w_out


def stick_breaking_block(q_blk, k_pre, v_pre, q_start):
    nq, nk = q_blk.shape[1], k_pre.shape[1]
    scale = 1.0 / math.sqrt(HEAD_DIM)
    z = jnp.einsum("bqhd,bkhd->bhqk", q_blk.astype(jnp.float32),
                   k_pre.astype(jnp.float32)) * scale
    t = q_start + jnp.arange(nq)
    s = jnp.arange(nk)
    mask = s[None, :] < t[:, None]
    sp = jnp.where(mask, jax.nn.softplus(z), 0.0)
    r = lax.cumsum(sp, axis=3, reverse=True)
    log_a = jnp.where(mask, z - r, -jnp.inf)
    a = jnp.exp(log_a)
    out = jnp.einsum("bhqk,bkhd->bqhd", a, v_pre.astype(jnp.float32))
    return out.astype(q_blk.dtype)


def stick_breaking_attention(q, k, v):
    L = q.shape[1]
    n_real_blocks = (L - N_META) // Q_BLOCK
    bounds = [(0, N_META)] + [(N_META + i * Q_BLOCK, N_META + (i + 1) * Q_BLOCK)
                              for i in range(n_real_blocks)]
    outs = []
    for (st, en) in bounds:
        outs.append(stick_breaking_block(q[:, st:en], k[:, :en], v[:, :en], st))
    return jnp.concatenate(outs, axis=1)


def setup_inputs(seed: int = 0) -> dict:
    key = jax.random.key(seed)
    ks = jax.random.split(key, 20)
    f32 = jnp.float32
    nrm = lambda k, shape, s: jax.random.normal(k, shape, f32) * s
    gain = lambda k, shape: 1.0 + 0.02 * jax.random.normal(k, shape, f32)
    return {
        "x": jax.random.normal(ks[0], (BATCH, SEQ, D_MODEL), f32),
        "meta_tokens": nrm(ks[1], (N_META, D_MODEL), 1.0),
        "pre_mix_g": gain(ks[2], (DEPTH, D_MODEL)),
        "w_in": nrm(ks[3], (DEPTH, D_MODEL, IN_W), D_MODEL ** -0.5),
        "gate_b": nrm(ks[4], (DEPTH, N_BRANCH * D_MODEL), 0.02),
        "dw_w": nrm(ks[5], (DEPTH, CONV_WIDTH, C_CONV), CONV_WIDTH ** -0.5),
        "dw_b": nrm(ks[6], (DEPTH, C_CONV), 0.02),
        "conv_ln_g": gain(ks[7], (DEPTH, C_CONV)),
        "conv_ln_b": nrm(ks[8], (DEPTH, C_CONV), 0.02),
        "w_conv_out": nrm(ks[9], (DEPTH, C_CONV, D_MODEL), C_CONV ** -0.5),
        "w_attn_out": nrm(ks[10], (DEPTH, ATTN_W, D_MODEL), ATTN_W ** -0.5),
        "w_o": nrm(ks[11], (DEPTH, D_MODEL, D_MODEL), D_MODEL ** -0.5),
        "post_mix_g": gain(ks[12], (DEPTH, D_MODEL)),
        "pre_ffn_g": gain(ks[13], (DEPTH, D_MODEL)),
        "w_ffn_in": nrm(ks[14], (DEPTH, D_MODEL, 2 * D_FF), D_MODEL ** -0.5),
        "w_ffn_out": nrm(ks[15], (DEPTH, D_FF, D_MODEL), D_FF ** -0.5),
        "post_ffn_g": gain(ks[16], (DEPTH, D_MODEL)),
    }


def reference(x, meta_tokens, pre_mix_g, w_in, gate_b, dw_w, dw_b, conv_ln_g, conv_ln_b,
              w_conv_out, w_attn_out, w_o, post_mix_g, pre_ffn_g, w_ffn_in, w_ffn_out,
              post_ffn_g):
    B = x.shape[0]
    meta = jnp.broadcast_to(meta_tokens[None].astype(x.dtype), (B, N_META, D_MODEL))
    h = jnp.concatenate([meta, x], axis=1)
    L = h.shape[1]
    for l in range(DEPTH):
        u = rms_norm(h, pre_mix_g[l])
        p = u @ w_in[l]
        o1 = 2 * C_CONV
        o2 = o1 + ATTN_W
        o3 = o2 + ATTN_W
        o4 = o3 + ATTN_W
        p_glu = p[..., :o1]
        q = p[..., o1:o2].reshape(B, L, N_HEADS, HEAD_DIM)
        k = p[..., o2:o3].reshape(B, L, N_HEADS, HEAD_DIM)
        v = p[..., o3:o4].reshape(B, L, N_HEADS, HEAD_DIM)
        gates = jax.nn.sigmoid(p[..., o4:] + gate_b[l])
        g_conv, g_attn = jnp.split(gates, 2, axis=-1)

        y_conv = conformer_conv(p_glu, dw_w[l], dw_b[l], conv_ln_g[l], conv_ln_b[l],
                                w_conv_out[l])
        y_attn = stick_breaking_attention(q, k, v).reshape(B, L, ATTN_W) @ w_attn_out[l]

        mix = (g_conv * y_conv + g_attn * y_attn) @ w_o[l]
        h = h + rms_norm(mix, post_mix_g[l])

        u = rms_norm(h, pre_ffn_g[l])
        a, b = jnp.split(u @ w_ffn_in[l], 2, axis=-1)
        f = (jax.nn.silu(a) * b) @ w_ffn_out[l]
        h = h + rms_norm(f, post_ffn_g[l])
    return h[:, N_META:]
```

```python
import functools

import jax
import jax.numpy as jnp
from jax import lax
from jax.experimental import pallas as pl
from jax.experimental.pallas import tpu as pltpu

D_MODEL = 1024
N_META = 16
C_CONV = D_MODEL
CONV_WIDTH = 31
N_HEADS = 16
HEAD_DIM = 64
ATTN_W = N_HEADS * HEAD_DIM
D_FF = 2816
RMS_EPS = 1e-6
LN_EPS = 1e-5

LANES = 128
N_HEAD_PAIRS = ATTN_W // LANES
O_GLU = 0
O_Q = 2 * C_CONV
O_K = O_Q + ATTN_W
O_V = O_K + ATTN_W
O_GATE = O_V + ATTN_W
IN_W = O_GATE + 2 * D_MODEL

IN_TM = 512
IN_NC = 512
ATT_TQ = 256
ATT_TK = 256
META_PAD = 128
POST_TM = 256
CONV_HALO = 32
CONV_RC = 64
FFN_TM = 512
FFN_NC = 256

F32 = jnp.float32
BF16 = jnp.bfloat16


def _sigmoid(x):
    return 1.0 / (1.0 + jnp.exp(-x))


def _rms_norm(x, g):
    return x * lax.rsqrt(jnp.mean(x * x, axis=-1, keepdims=True) + RMS_EPS) * g


def _resident(shape):
    nd = len(shape)
    return pl.BlockSpec(shape, lambda *_: (0,) * nd, pipeline_mode=pl.Buffered(1))


def _in_proj_kernel(x_ref, g_ref, w_ref, gb_ref, uc_ref, q_ref, k_ref, v_ref, gate_ref):
    u = _rms_norm(x_ref[...], g_ref[...]).astype(BF16)

    def proj(off, c):
        return jnp.dot(u, w_ref[:, off + c:off + c + IN_NC], preferred_element_type=F32)

    for c in range(0, C_CONV, IN_NC):
        uc_ref[:, c:c + IN_NC] = proj(O_GLU, c) * _sigmoid(proj(O_GLU + C_CONV, c))

    for ref, off, scale in ((q_ref, O_Q, HEAD_DIM ** -0.5), (k_ref, O_K, None), (v_ref, O_V, None)):
        for c in range(0, ATTN_W, IN_NC):
            y = proj(off, c)
            if scale is not None:
                y = y * scale
            y = y.astype(BF16)
            for s in range(IN_NC // LANES):
                ref[c // LANES + s] = y[:, s * LANES:(s + 1) * LANES]

    for c in range(0, 2 * D_MODEL, IN_NC):
        gate_ref[:, c:c + IN_NC] = _sigmoid(proj(O_GATE, c) + gb_ref[:, c:c + IN_NC]).astype(BF16)


def _in_proj(x2d, g, w_bf16, gate_b, tm):
    rows = x2d.shape[0]
    hp_spec = pl.BlockSpec((N_HEAD_PAIRS, tm, LANES), lambda i: (0, i, 0))
    hp_shape = jax.ShapeDtypeStruct((N_HEAD_PAIRS, rows, LANES), BF16)
    return pl.pallas_call(
        _in_proj_kernel,
        grid=(rows // tm,),
        in_specs=[
            pl.BlockSpec((tm, D_MODEL), lambda i: (i, 0)),
            _resident((1, D_MODEL)),
            _resident((D_MODEL, IN_W)),
            _resident((1, 2 * D_MODEL)),
        ],
        out_specs=[
            pl.BlockSpec((tm, C_CONV), lambda i: (i, 0)),
            hp_spec, hp_spec, hp_spec,
            pl.BlockSpec((tm, 2 * D_MODEL), lambda i: (i, 0)),
        ],
        out_shape=[
            jax.ShapeDtypeStruct((rows, C_CONV), F32),
            hp_shape, hp_shape, hp_shape,
            jax.ShapeDtypeStruct((rows, 2 * D_MODEL), BF16),
        ],
        compiler_params=pltpu.CompilerParams(
            dimension_semantics=("arbitrary",), vmem_limit_bytes=48 * 1024 * 1024),
        name="in_proj",
    )(x2d, g, w_bf16, gate_b)


def _attn_kernel(q_ref, k_ref, v_ref, km_ref, vm_ref, o_ref, acc_ref, carry_ref):
    tq, tk = ATT_TQ, ATT_TK
    i = pl.program_id(2)

    q = q_ref[...]
    lane = lax.broadcasted_iota(jnp.int32, (tq, LANES), 1)
    zero = jnp.zeros_like(q)
    q2 = jnp.concatenate([jnp.where(lane < HEAD_DIM, q, zero),
                          jnp.where(lane >= HEAD_DIM, q, zero)], axis=0)

    tri = (lax.broadcasted_iota(jnp.int32, (tk, tk), 0)
           >= lax.broadcasted_iota(jnp.int32, (tk, tk), 1)).astype(BF16)

    def sweep(k, v, tri_kk, mask):
        z = lax.dot_general(q2, k, (((1,), (1,)), ((), ())), preferred_element_type=F32)
        sp = jnp.maximum(z, 0.0) + jnp.log(1.0 + jnp.exp(-jnp.abs(z)))
        if mask is not None:
            sp = jnp.where(mask, sp, 0.0)
        hi = sp.astype(BF16)
        lo = (sp - hi.astype(F32)).astype(BF16)
        r2 = jnp.dot(jnp.concatenate([hi, lo], axis=0), tri_kk, preferred_element_type=F32)
        r = r2[:2 * tq] + r2[2 * tq:] + carry_ref[...]
        a = jnp.exp(z - r)
        if mask is not None:
            a = jnp.where(mask, a, 0.0)
        acc_ref[...] += jnp.dot(a.astype(BF16), v, preferred_element_type=F32)
        carry_ref[...] = r[:, 0:1]

    acc_ref[...] = jnp.zeros_like(acc_ref)
    carry_ref[...] = jnp.zeros_like(carry_ref)

    row = lax.broadcasted_iota(jnp.int32, (2 * tq, tk), 0) & (tq - 1)
    col = lax.broadcasted_iota(jnp.int32, (2 * tq, tk), 1)
    d0 = pl.multiple_of(i * tk, tk)
    sweep(k_ref[pl.ds(d0, tk), :], v_ref[pl.ds(d0, tk), :], tri, col < row)

    def body(n, _):
        j0 = pl.multiple_of((i - 1 - n) * tk, tk)
        sweep(k_ref[pl.ds(j0, tk), :], v_ref[pl.ds(j0, tk), :], tri, None)
        return 0

    lax.fori_loop(0, i, body, 0)

    colm = lax.broadcasted_iota(jnp.int32, (2 * tq, META_PAD), 1)
    sweep(km_ref[...], vm_ref[...], tri[:META_PAD, :META_PAD], colm < N_META)

    acc = acc_ref[...]
    o_ref[...] = jnp.where(lane < HEAD_DIM, acc[:tq], acc[tq:]).astype(o_ref.dtype)


def _attention(q, k, v, km, vm, batch, seq):
    nq = seq // ATT_TQ
    return pl.pallas_call(
        _attn_kernel,
        grid=(N_HEAD_PAIRS, batch, nq),
        in_specs=[
            pl.BlockSpec((None, ATT_TQ, LANES), lambda h, b, i: (h, b * nq + i, 0)),
            pl.BlockSpec((None, seq, LANES), lambda h, b, i: (h, b, 0)),
            pl.BlockSpec((None, seq, LANES), lambda h, b, i: (h, b, 0)),
            pl.BlockSpec((None, META_PAD, LANES), lambda h, b, i: (h, 0, 0)),
            pl.BlockSpec((None, META_PAD, LANES), lambda h, b, i: (h, 0, 0)),
        ],
        out_specs=pl.BlockSpec((ATT_TQ, LANES), lambda h, b, i: (b * nq + i, h)),
        out_shape=jax.ShapeDtypeStruct((batch * seq, ATTN_W), BF16),
        scratch_shapes=[
            pltpu.VMEM((2 * ATT_TQ, LANES), F32),
            pltpu.VMEM((2 * ATT_TQ, 1), F32),
        ],
        compiler_params=pltpu.CompilerParams(
            dimension_semantics=("arbitrary", "arbitrary", "arbitrary"),
            vmem_limit_bytes=40 * 1024 * 1024),
        name="sb_attn",
    )(q, k, v, km, vm)


def _post_kernel(tiles_per_batch,
                 uc_ref, halo_ref, ucm_ref, o_ref, gate_ref, x_ref,
                 dww_ref, dwb_ref, lng_ref, lnb_ref, wc_ref, wa_ref, wo_ref, pg_ref,
                 h1_ref, win_ref, y_ref):
    tm = POST_TM
    t = pl.program_id(0) % tiles_per_batch

    win_ref[CONV_HALO:, :] = uc_ref[...]

    @pl.when(t == 0)
    def _():
        win_ref[0:CONV_HALO - N_META, :] = jnp.zeros((CONV_HALO - N_META, C_CONV), F32)
        win_ref[CONV_HALO - N_META:CONV_HALO, :] = ucm_ref[...]

    @pl.when(t != 0)
    def _():
        win_ref[0:CONV_HALO, :] = halo_ref[...]

    base = CONV_HALO - (CONV_WIDTH - 1)
    for cb in range(C_CONV // LANES):
        cs = slice(cb * LANES, (cb + 1) * LANES)
        for rc in range(0, tm, CONV_RC):
            acc = win_ref[rc + base:rc + base + CONV_RC, cs] * dww_ref[0:1, cs]
            for w in range(1, CONV_WIDTH):
                acc = acc + win_ref[rc + base + w:rc + base + w + CONV_RC, cs] * dww_ref[w:w + 1, cs]
            y_ref[rc:rc + CONV_RC, cs] = acc + dwb_ref[:, cs]

    y = y_ref[...]
    mu = jnp.mean(y, axis=-1, keepdims=True)
    yc = y - mu
    var = jnp.mean(yc * yc, axis=-1, keepdims=True)
    n = yc * lax.rsqrt(var + LN_EPS) * lng_ref[...] + lnb_ref[...]
    c = (n * _sigmoid(n)).astype(BF16)

    y_conv = jnp.dot(c, wc_ref[...], preferred_element_type=F32)
    y_attn = jnp.dot(o_ref[...], wa_ref[...], preferred_element_type=F32)
    g_conv = gate_ref[:, :D_MODEL].astype(F32)
    g_attn = gate_ref[:, D_MODEL:].astype(F32)
    m = (g_conv * y_conv + g_attn * y_attn).astype(BF16)
    mix = jnp.dot(m, wo_ref[...], preferred_element_type=F32)
    h1_ref[...] = x_ref[...] + _rms_norm(mix, pg_ref[...])


def _post(uc, uc_meta, o, gates, x2d, dw_w, dw_b, ln_g, ln_b, wc, wa, wo, pg, seq):
    rows = x2d.shape[0]
    tm = POST_TM
    tiles_per_batch = seq // tm
    halo_per_tile = tm // CONV_HALO
    row_spec = lambda width: pl.BlockSpec((tm, width), lambda i: (i, 0))
    return pl.pallas_call(
        functools.partial(_post_kernel, tiles_per_batch),
        grid=(rows // tm,),
        in_specs=[
            row_spec(C_CONV),
            pl.BlockSpec((CONV_HALO, C_CONV), lambda i: (jnp.maximum(i * halo_per_tile - 1, 0), 0)),
            _resident((N_META, C_CONV)),
            row_spec(ATTN_W),
            row_spec(2 * D_MODEL),
            row_spec(D_MODEL),
            _resident((CONV_WIDTH, C_CONV)),
            _resident((1, C_CONV)),
            _resident((1, C_CONV)),
            _resident((1, C_CONV)),
            _resident((C_CONV, D_MODEL)),
            _resident((ATTN_W, D_MODEL)),
            _resident((D_MODEL, D_MODEL)),
            _resident((1, D_MODEL)),
        ],
        out_specs=row_spec(D_MODEL),
        out_shape=jax.ShapeDtypeStruct((rows, D_MODEL), F32),
        scratch_shapes=[
            pltpu.VMEM((CONV_HALO + tm, C_CONV), F32),
            pltpu.VMEM((tm, C_CONV), F32),
        ],
        compiler_params=pltpu.CompilerParams(
            dimension_semantics=("arbitrary",), vmem_limit_bytes=40 * 1024 * 1024),
        name="post_mix",
    )(uc, uc, uc_meta, o, gates, x2d, dw_w, dw_b, ln_g, ln_b, wc, wa, wo, pg)


def _ffn_kernel(h_ref, g1_ref, wi_ref, wout_ref, g2_ref, out_ref, acc_ref):
    h = h_ref[...]
    u = _rms_norm(h, g1_ref[...]).astype(BF16)
    for n, c in enumerate(range(0, D_FF, FFN_NC)):
        a = jnp.dot(u, wi_ref[:, c:c + FFN_NC], preferred_element_type=F32)
        b = jnp.dot(u, wi_ref[:, D_FF + c:D_FF + c + FFN_NC], preferred_element_type=F32)
        s = (a * _sigmoid(a) * b).astype(BF16)
        part = jnp.dot(s, wout_ref[c:c + FFN_NC, :], preferred_element_type=F32)
        if n == 0:
            acc_ref[...] = part
        else:
            acc_ref[...] += part
    out_ref[...] = h + _rms_norm(acc_ref[...], g2_ref[...])


def _ffn(h1, g1, wi, wout, g2):
    rows = h1.shape[0]
    tm = FFN_TM
    return pl.pallas_call(
        _ffn_kernel,
        grid=(rows // tm,),
        in_specs=[
            pl.BlockSpec((tm, D_MODEL), lambda i: (i, 0)),
            _resident((1, D_MODEL)),
            _resident((D_MODEL, 2 * D_FF)),
            _resident((D_FF, D_MODEL)),
            _resident((1, D_MODEL)),
        ],
        out_specs=pl.BlockSpec((tm, D_MODEL), lambda i: (i, 0)),
        out_shape=jax.ShapeDtypeStruct((rows, D_MODEL), F32),
        scratch_shapes=[pltpu.VMEM((tm, D_MODEL), F32)],
        compiler_params=pltpu.CompilerParams(
            dimension_semantics=("arbitrary",), vmem_limit_bytes=52 * 1024 * 1024),
        name="ffn",
    )(h1, g1, wi, wout, g2)


def kernel(x, meta_tokens, pre_mix_g, w_in, gate_b, dw_w, dw_b, conv_ln_g, conv_ln_b,
           w_conv_out, w_attn_out, w_o, post_mix_g, pre_ffn_g, w_ffn_in, w_ffn_out,
           post_ffn_g):
    batch, seq, d = x.shape
    assert d == D_MODEL and seq % ATT_TQ == 0 and seq % POST_TM == 0
    assert w_in.shape[0] == 1, "single layer"
    rows = batch * seq
    assert rows % IN_TM == 0 and rows % FFN_TM == 0

    row = lambda p: p[0].reshape(1, -1)
    x2d = x.reshape(rows, d)
    w_in_b = w_in[0].astype(BF16)

    uc, q, k, v, gates = _in_proj(x2d, row(pre_mix_g), w_in_b, row(gate_b), IN_TM)
    uc_m, _, k_m, v_m, _ = _in_proj(meta_tokens.astype(x.dtype), row(pre_mix_g), w_in_b,
                                    row(gate_b), N_META)
    pad = ((0, 0), (0, META_PAD - N_META), (0, 0))
    o = _attention(q, k, v, jnp.pad(k_m, pad), jnp.pad(v_m, pad), batch, seq)

    h1 = _post(uc, uc_m, o, gates, x2d, dw_w[0], row(dw_b), row(conv_ln_g), row(conv_ln_b),
               w_conv_out[0].astype(BF16), w_attn_out[0].astype(BF16), w_o[0].astype(BF16),
               row(post_mix_g), seq)
    out = _ffn(h1, row(pre_ffn_g), w_ffn_in[0].astype(BF16), w_ffn_out[0].astype(BF16),
               row(post_ffn_g))
    return out.reshape(batch, seq, d)
```

```python
import functools

import jax
import jax.numpy as jnp
from jax import lax
from jax.experimental import pallas as pl
from jax.experimental.pallas import tpu as pltpu

D_MODEL = 1024
N_META = 16
C_CONV = D_MODEL
CONV_WIDTH = 31
N_HEADS = 16
HEAD_DIM = 64
ATTN_W = N_HEADS * HEAD_DIM
D_FF = 2816
RMS_EPS = 1e-6
LN_EPS = 1e-5

LANES = 128
SUBLANES = 8
N_HEAD_PAIRS = ATTN_W // LANES
O_GLU = 0
O_Q = 2 * C_CONV
O_K = O_Q + ATTN_W
O_V = O_K + ATTN_W
O_GATE = O_V + ATTN_W
IN_W = O_GATE + 2 * D_MODEL

IN_TM = 512
IN_NC = 512
ATT_TQ = 256
ATT_TK = 256
ATT_NP = 4
META_PAD = 128
LOG2_E = 1.4426950408889634
ATT_EXIT = 151.0
POST_TM = 512
CONV_HALO = 32
CONV_RC = 64
FFN_TM = 512
FFN_NC = 256

F32 = jnp.float32
BF16 = jnp.bfloat16


def _sigmoid(x):
    return 1.0 / (1.0 + jnp.exp(-x))


def _rms_norm(x, g):
    return x * lax.rsqrt(jnp.mean(x * x, axis=-1, keepdims=True) + RMS_EPS) * g


def _resident(shape):
    nd = len(shape)
    return pl.BlockSpec(shape, lambda *_: (0,) * nd, pipeline_mode=pl.Buffered(1))


def _proj_jobs(u, w_ref, gb_ref, glu_store, q_ref, k_ref, v_ref, gate_ref, nc):
    def proj(off, c):
        return jnp.dot(u, w_ref[:, off + c:off + c + nc], preferred_element_type=F32)

    def glu(c):
        glu_store(c, proj(O_GLU, c) * _sigmoid(proj(O_GLU + C_CONV, c)))

    def heads(ref, off, scale, c):
        y = proj(off, c)
        if scale is not None:
            y = y * scale
        y = y.astype(BF16)
        for s in range(nc // LANES):
            ref[c // LANES + s] = y[:, s * LANES:(s + 1) * LANES]

    def gate(c):
        gate_ref[:, c:c + nc] = _sigmoid(proj(O_GATE, c) + gb_ref[:, c:c + nc]).astype(BF16)

    glu_jobs = [functools.partial(glu, c) for c in range(0, C_CONV, nc)]
    rest = []
    for ref, off, scale in ((q_ref, O_Q, LOG2_E * HEAD_DIM ** -0.5), (k_ref, O_K, None), (v_ref, O_V, None)):
        if ref is not None:
            rest += [functools.partial(heads, ref, off, scale, c) for c in range(0, ATTN_W, nc)]
    if gate_ref is not None:
        rest += [functools.partial(gate, c) for c in range(0, 2 * D_MODEL, nc)]
    return glu_jobs, rest


def _meta_proj_kernel(x_ref, g_ref, w_ref, uc_ref, k_ref, v_ref):
    u = _rms_norm(x_ref[...], g_ref[...]).astype(BF16)

    def glu_store(c, val):
        uc_ref[:, c:c + IN_NC] = val

    glu_jobs, rest = _proj_jobs(u, w_ref, None, glu_store, None, k_ref, v_ref, None, IN_NC)
    for job in glu_jobs + rest:
        job()


def _meta_proj(meta, g, w_bf16):
    hp_shape = jax.ShapeDtypeStruct((N_HEAD_PAIRS, N_META, LANES), BF16)
    return pl.pallas_call(
        _meta_proj_kernel,
        grid=(1,),
        in_specs=[_resident((N_META, D_MODEL)), _resident((1, D_MODEL)), _resident((D_MODEL, IN_W))],
        out_specs=[pl.BlockSpec((N_META, C_CONV), lambda i: (0, 0)),
                   pl.BlockSpec((N_HEAD_PAIRS, N_META, LANES), lambda i: (0, 0, 0)),
                   pl.BlockSpec((N_HEAD_PAIRS, N_META, LANES), lambda i: (0, 0, 0))],
        out_shape=[jax.ShapeDtypeStruct((N_META, C_CONV), F32), hp_shape, hp_shape],
        compiler_params=pltpu.CompilerParams(
            dimension_semantics=("arbitrary",), vmem_limit_bytes=48 * 1024 * 1024),
        name="meta_proj",
    )(meta, g, w_bf16)


def _conv_block(win_ref, dww_ref, dwb_ref, y_ref, cb, nrows):
    base = CONV_HALO - (CONV_WIDTH - 1)
    cs = slice(cb * LANES, (cb + 1) * LANES)
    for rc in range(0, nrows, CONV_RC):
        total = None
        for phase in range(SUBLANES):
            acc = None
            for w in range(CONV_WIDTH):
                if (base + w) % SUBLANES != phase:
                    continue
                term = win_ref[rc + base + w:rc + base + w + CONV_RC, cs] * dww_ref[w:w + 1, cs]
                acc = term if acc is None else acc + term
            total = acc if total is None else total + acc
        y_ref[rc:rc + CONV_RC, cs] = total + dwb_ref[:, cs]


def _in_proj_kernel(tiles_per_batch, x_ref, g_ref, w_ref, gb_ref, ucm_ref, dww_ref, dwb_ref,
                    y_ref, q_ref, k_ref, v_ref, gate_ref, win_ref):
    tm = IN_TM
    i = pl.program_id(0)

    @pl.when(i % tiles_per_batch == 0)
    def _():
        win_ref[0:CONV_HALO - N_META, :] = jnp.zeros((CONV_HALO - N_META, C_CONV), F32)
        win_ref[CONV_HALO - N_META:CONV_HALO, :] = ucm_ref[...]

    @pl.when(i % tiles_per_batch != 0)
    def _():
        win_ref[0:CONV_HALO, :] = win_ref[tm:tm + CONV_HALO, :]

    u = _rms_norm(x_ref[...], g_ref[...]).astype(BF16)

    def glu_store(c, val):
        win_ref[CONV_HALO:, c:c + IN_NC] = val

    glu_jobs, rest = _proj_jobs(u, w_ref, gb_ref, glu_store, q_ref, k_ref, v_ref, gate_ref, IN_NC)
    for job in glu_jobs:
        job()
    conv_blocks = list(range(C_CONV // LANES))
    for job in rest:
        cb = conv_blocks.pop(0) if conv_blocks else None

        @pl.when(i >= 0)
        def _(job=job, cb=cb):
            job()
            if cb is not None:
                _conv_block(win_ref, dww_ref, dwb_ref, y_ref, cb, tm)
    assert not conv_blocks


def _in_proj(x2d, g, w_bf16, gate_b, uc_meta, dw_w, dw_b, seq):
    rows = x2d.shape[0]
    tm = IN_TM
    hp_spec = pl.BlockSpec((N_HEAD_PAIRS, tm, LANES), lambda i: (0, i, 0))
    hp_shape = jax.ShapeDtypeStruct((N_HEAD_PAIRS, rows, LANES), BF16)
    return pl.pallas_call(
        functools.partial(_in_proj_kernel, seq // tm),
        grid=(rows // tm,),
        in_specs=[
            pl.BlockSpec((tm, D_MODEL), lambda i: (i, 0)),
            _resident((1, D_MODEL)),
            _resident((D_MODEL, IN_W)),
            _resident((1, 2 * D_MODEL)),
            _resident((N_META, C_CONV)),
            _resident((CONV_WIDTH, C_CONV)),
            _resident((1, C_CONV)),
        ],
        out_specs=[
            pl.BlockSpec((tm, C_CONV), lambda i: (i, 0)),
            hp_spec, hp_spec, hp_spec,
            pl.BlockSpec((tm, 2 * D_MODEL), lambda i: (i, 0)),
        ],
        out_shape=[
            jax.ShapeDtypeStruct((rows, C_CONV), F32),
            hp_shape, hp_shape, hp_shape,
            jax.ShapeDtypeStruct((rows, 2 * D_MODEL), BF16),
        ],
        scratch_shapes=[pltpu.VMEM((CONV_HALO + tm, C_CONV), F32)],
        compiler_params=pltpu.CompilerParams(
            dimension_semantics=("arbitrary",), vmem_limit_bytes=48 * 1024 * 1024),
        name="in_proj",
    )(x2d, g, w_bf16, gate_b, uc_meta, dw_w, dw_b)


def _attn_kernel(q_ref, k_ref, v_ref, km_ref, vm_ref, o_ref, acc_ref, carry_ref):
    tq, tk = ATT_TQ, ATT_TK
    i = pl.program_id(2)
    chains = [(p, h) for p in range(ATT_NP) for h in range(2)]

    lane = lax.broadcasted_iota(jnp.int32, (tq, LANES), 1)
    in_head = (lane < HEAD_DIM, lane >= HEAD_DIM)
    qh = [jnp.where(in_head[h], q_ref[p], jnp.zeros((tq, LANES), BF16)) for p, h in chains]

    def tri2(n):
        t = (lax.broadcasted_iota(jnp.int32, (n, n), 0)
             >= lax.broadcasted_iota(jnp.int32, (n, n), 1)).astype(BF16)
        return jnp.concatenate([t, t], axis=0)

    def sweep(ks, vs, tri2_kk, mask):
        nc = len(chains)
        z = [lax.dot_general(qh[c], ks[p], (((1,), (1,)), ((), ())), preferred_element_type=F32)
             for c, (p, h) in enumerate(chains)]
        hl = []
        for c in range(nc):
            sp = jnp.maximum(z[c], 0.0) + jnp.log2(1.0 + jnp.exp2(-jnp.abs(z[c])))
            if mask is not None:
                sp = jnp.where(mask, sp, 0.0)
            hi = sp.astype(BF16)
            lo = (sp - hi.astype(F32)).astype(BF16)
            hl.append(jnp.concatenate([hi, lo], axis=1))
        r = [jnp.dot(hl[c], tri2_kk, preferred_element_type=F32) + carry_ref[c] for c in range(nc)]
        a = []
        for c in range(nc):
            w = jnp.exp2(z[c] - r[c])
            if mask is not None:
                w = jnp.where(mask, w, 0.0)
            a.append(w.astype(BF16))
        for c, (p, h) in enumerate(chains):
            acc_ref[c] += jnp.dot(a[c], vs[p], preferred_element_type=F32)
            carry_ref[c] = r[c][:, 0:1]

    def real_tile(j, mask):
        j0 = pl.multiple_of(j * tk, tk)
        sweep([k_ref[p, pl.ds(j0, tk), :] for p in range(ATT_NP)],
              [v_ref[p, pl.ds(j0, tk), :] for p in range(ATT_NP)], tri2(tk), mask)

    def more_keys_matter():
        return (jnp.min(carry_ref[...]) < ATT_EXIT).astype(jnp.int32)

    acc_ref[...] = jnp.zeros_like(acc_ref)
    carry_ref[...] = jnp.zeros_like(carry_ref)

    real_tile(i, lax.broadcasted_iota(jnp.int32, (tq, tk), 1)
              < lax.broadcasted_iota(jnp.int32, (tq, tk), 0))

    def body(state):
        n, _ = state
        real_tile(i - 1 - n, None)
        return n + 1, more_keys_matter()

    _, go = lax.while_loop(lambda s: jnp.logical_and(s[0] < i, s[1] > 0), body,
                           (jnp.int32(0), more_keys_matter()))

    @pl.when(go > 0)
    def _():
        colm = lax.broadcasted_iota(jnp.int32, (tq, META_PAD), 1)
        sweep([km_ref[p] for p in range(ATT_NP)], [vm_ref[p] for p in range(ATT_NP)],
              tri2(META_PAD), colm < N_META)

    for p in range(ATT_NP):
        o_ref[:, p * LANES:(p + 1) * LANES] = jnp.where(
            in_head[0], acc_ref[2 * p], acc_ref[2 * p + 1]).astype(o_ref.dtype)


def _attention(q, k, v, km, vm, batch, seq):
    nq = seq // ATT_TQ
    np_ = ATT_NP
    return pl.pallas_call(
        _attn_kernel,
        grid=(N_HEAD_PAIRS // np_, batch, nq),
        in_specs=[
            pl.BlockSpec((np_, ATT_TQ, LANES), lambda g, b, i: (g, b * nq + i, 0)),
            pl.BlockSpec((np_, seq, LANES), lambda g, b, i: (g, b, 0)),
            pl.BlockSpec((np_, seq, LANES), lambda g, b, i: (g, b, 0)),
            pl.BlockSpec((np_, META_PAD, LANES), lambda g, b, i: (g, 0, 0)),
            pl.BlockSpec((np_, META_PAD, LANES), lambda g, b, i: (g, 0, 0)),
        ],
        out_specs=pl.BlockSpec((ATT_TQ, np_ * LANES), lambda g, b, i: (b * nq + i, g)),
        out_shape=jax.ShapeDtypeStruct((batch * seq, ATTN_W), BF16),
        scratch_shapes=[
            pltpu.VMEM((2 * np_, ATT_TQ, LANES), F32),
            pltpu.VMEM((2 * np_, ATT_TQ, 1), F32),
        ],
        compiler_params=pltpu.CompilerParams(
            dimension_semantics=("arbitrary", "arbitrary", "arbitrary"),
            vmem_limit_bytes=40 * 1024 * 1024),
        name="sb_attn",
    )(q, k, v, km, vm)


def _post_kernel(y_ref, o_ref, gate_ref, x_ref, lng_ref, lnb_ref, wc_ref, wa_ref, wo_ref, pg_ref,
                 h1_ref):
    y_attn = jnp.dot(o_ref[...], wa_ref[...], preferred_element_type=F32)

    y = y_ref[...]
    mu = jnp.mean(y, axis=-1, keepdims=True)
    yc = y - mu
    var = jnp.mean(yc * yc, axis=-1, keepdims=True)
    n = yc * lax.rsqrt(var + LN_EPS) * lng_ref[...] + lnb_ref[...]
    c = (n * _sigmoid(n)).astype(BF16)

    y_conv = jnp.dot(c, wc_ref[...], preferred_element_type=F32)
    g_conv = gate_ref[:, :D_MODEL].astype(F32)
    g_attn = gate_ref[:, D_MODEL:].astype(F32)
    m = (g_conv * y_conv + g_attn * y_attn).astype(BF16)
    mix = jnp.dot(m, wo_ref[...], preferred_element_type=F32)
    h1_ref[...] = x_ref[...] + _rms_norm(mix, pg_ref[...])


def _post(y, o, gates, x2d, ln_g, ln_b, wc, wa, wo, pg):
    rows = x2d.shape[0]
    tm = POST_TM
    row_spec = lambda width: pl.BlockSpec((tm, width), lambda i: (i, 0))
    return pl.pallas_call(
        _post_kernel,
        grid=(rows // tm,),
        in_specs=[
            row_spec(C_CONV),
            row_spec(ATTN_W),
            row_spec(2 * D_MODEL),
            row_spec(D_MODEL),
            _resident((1, C_CONV)),
            _resident((1, C_CONV)),
            _resident((C_CONV, D_MODEL)),
            _resident((ATTN_W, D_MODEL)),
            _resident((D_MODEL, D_MODEL)),
            _resident((1, D_MODEL)),
        ],
        out_specs=row_spec(D_MODEL),
        out_shape=jax.ShapeDtypeStruct((rows, D_MODEL), F32),
        compiler_params=pltpu.CompilerParams(
            dimension_semantics=("arbitrary",), vmem_limit_bytes=40 * 1024 * 1024),
        name="post_mix",
    )(y, o, gates, x2d, ln_g, ln_b, wc, wa, wo, pg)


def _ffn_kernel(h_ref, g1_ref, wi_ref, wout_ref, g2_ref, out_ref, acc_ref):
    h = h_ref[...]
    u = _rms_norm(h, g1_ref[...]).astype(BF16)
    for n, c in enumerate(range(0, D_FF, FFN_NC)):
        a = jnp.dot(u, wi_ref[:, c:c + FFN_NC], preferred_element_type=F32)
        b = jnp.dot(u, wi_ref[:, D_FF + c:D_FF + c + FFN_NC], preferred_element_type=F32)
        s = (a * _sigmoid(a) * b).astype(BF16)
        part = jnp.dot(s, wout_ref[c:c + FFN_NC, :], preferred_element_type=F32)
        if n == 0:
            acc_ref[...] = part
        else:
            acc_ref[...] += part
    out_ref[...] = h + _rms_norm(acc_ref[...], g2_ref[...])


def _ffn(h1, g1, wi, wout, g2):
    rows = h1.shape[0]
    tm = FFN_TM
    return pl.pallas_call(
        _ffn_kernel,
        grid=(rows // tm,),
        in_specs=[
            pl.BlockSpec((tm, D_MODEL), lambda i: (i, 0)),
            _resident((1, D_MODEL)),
            _resident((D_MODEL, 2 * D_FF)),
            _resident((D_FF, D_MODEL)),
            _resident((1, D_MODEL)),
        ],
        out_specs=pl.BlockSpec((tm, D_MODEL), lambda i: (i, 0)),
        out_shape=jax.ShapeDtypeStruct((rows, D_MODEL), F32),
        scratch_shapes=[pltpu.VMEM((tm, D_MODEL), F32)],
        compiler_params=pltpu.CompilerParams(
            dimension_semantics=("arbitrary",), vmem_limit_bytes=52 * 1024 * 1024),
        name="ffn",
    )(h1, g1, wi, wout, g2)


def kernel(x, meta_tokens, pre_mix_g, w_in, gate_b, dw_w, dw_b, conv_ln_g, conv_ln_b,
           w_conv_out, w_attn_out, w_o, post_mix_g, pre_ffn_g, w_ffn_in, w_ffn_out,
           post_ffn_g):
    batch, seq, d = x.shape
    assert d == D_MODEL and seq % ATT_TQ == 0 and seq % IN_TM == 0
    assert w_in.shape[0] == 1, "single layer"
    rows = batch * seq
    assert rows % POST_TM == 0 and rows % FFN_TM == 0

    row = lambda p: p[0].reshape(1, -1)
    x2d = x.reshape(rows, d)
    w_in_b = w_in[0].astype(BF16)

    uc_m, k_m, v_m = _meta_proj(meta_tokens.astype(x.dtype), row(pre_mix_g), w_in_b)
    y, q, k, v, gates = _in_proj(x2d, row(pre_mix_g), w_in_b, row(gate_b), uc_m,
                                 dw_w[0], row(dw_b), seq)
    pad = ((0, 0), (0, META_PAD - N_META), (0, 0))
    o = _attention(q, k, v, jnp.pad(k_m, pad), jnp.pad(v_m, pad), batch, seq)

    h1 = _post(y, o, gates, x2d, row(conv_ln_g), row(conv_ln_b),
               w_conv_out[0].astype(BF16), w_attn_out[0].astype(BF16), w_o[0].astype(BF16),
               row(post_mix_g))
    out = _ffn(h1, row(pre_ffn_g), w_ffn_in[0].astype(BF16), w_ffn_out[0].astype(BF16),
               row(post_ffn_g))
    return out.reshape(batch, seq, d)
```

```python
import functools

import jax
import jax.numpy as jnp
from jax import lax
from jax.experimental import pallas as pl
from jax.experimental.pallas import tpu as pltpu

D_MODEL = 1024
N_META = 16
C_CONV = D_MODEL
CONV_WIDTH = 31
N_HEADS = 16
HEAD_DIM = 64
ATTN_W = N_HEADS * HEAD_DIM
D_FF = 2816
RMS_EPS = 1e-6
LN_EPS = 1e-5

LANES = 128
SUBLANES = 8
N_HEAD_PAIRS = ATTN_W // LANES
O_GLU = 0
O_Q = 2 * C_CONV
O_K = O_Q + ATTN_W
O_V = O_K + ATTN_W
O_GATE = O_V + ATTN_W
IN_W = O_GATE + 2 * D_MODEL

IN_TM = 512
IN_NC = 512
ATT_TQ = 256
ATT_TK = 256
ATT_NP = 8
META_PAD = 128
LOG2_E = 1.4426950408889634
ATT_EXIT = 151.0
POST_TM = 512
CONV_HALO = 32
CONV_RC = 64
FFN_TM = 512
FFN_NC = 256

F32 = jnp.float32
BF16 = jnp.bfloat16


def _sigmoid(x):
    return 1.0 / (1.0 + jnp.exp(-x))


def _rms_norm(x, g):
    return x * lax.rsqrt(jnp.mean(x * x, axis=-1, keepdims=True) + RMS_EPS) * g


def _resident(shape):
    nd = len(shape)
    return pl.BlockSpec(shape, lambda *_: (0,) * nd, pipeline_mode=pl.Buffered(1))


def _proj_jobs(u, w_ref, gb_ref, glu_store, q_ref, k_ref, v_ref, gate_ref, nc):
    def proj(off, c):
        return jnp.dot(u, w_ref[:, off + c:off + c + nc], preferred_element_type=F32)

    def glu(c):
        glu_store(c, proj(O_GLU, c) * _sigmoid(proj(O_GLU + C_CONV, c)))

    def heads(ref, off, scale, c):
        y = proj(off, c)
        if scale is not None:
            y = y * scale
        y = y.astype(BF16)
        for s in range(nc // LANES):
            ref[c // LANES + s] = y[:, s * LANES:(s + 1) * LANES]

    def gate(c):
        gate_ref[:, c:c + nc] = _sigmoid(proj(O_GATE, c) + gb_ref[:, c:c + nc]).astype(BF16)

    glu_jobs = [functools.partial(glu, c) for c in range(0, C_CONV, nc)]
    rest = []
    for ref, off, scale in ((q_ref, O_Q, LOG2_E * HEAD_DIM ** -0.5), (k_ref, O_K, None), (v_ref, O_V, None)):
        if ref is not None:
            rest += [functools.partial(heads, ref, off, scale, c) for c in range(0, ATTN_W, nc)]
    if gate_ref is not None:
        rest += [functools.partial(gate, c) for c in range(0, 2 * D_MODEL, nc)]
    return glu_jobs, rest


def _meta_proj_kernel(x_ref, g_ref, w_ref, uc_ref, k_ref, v_ref):
    u = _rms_norm(x_ref[...], g_ref[...]).astype(BF16)

    def glu_store(c, val):
        uc_ref[:, c:c + IN_NC] = val

    glu_jobs, rest = _proj_jobs(u, w_ref, None, glu_store, None, k_ref, v_ref, None, IN_NC)
    for job in glu_jobs + rest:
        job()


def _meta_proj(meta, g, w_bf16):
    hp_shape = jax.ShapeDtypeStruct((N_HEAD_PAIRS, N_META, LANES), BF16)
    return pl.pallas_call(
        _meta_proj_kernel,
        grid=(1,),
        in_specs=[_resident((N_META, D_MODEL)), _resident((1, D_MODEL)), _resident((D_MODEL, IN_W))],
        out_specs=[pl.BlockSpec((N_META, C_CONV), lambda i: (0, 0)),
                   pl.BlockSpec((N_HEAD_PAIRS, N_META, LANES), lambda i: (0, 0, 0)),
                   pl.BlockSpec((N_HEAD_PAIRS, N_META, LANES), lambda i: (0, 0, 0))],
        out_shape=[jax.ShapeDtypeStruct((N_META, C_CONV), F32), hp_shape, hp_shape],
        compiler_params=pltpu.CompilerParams(
            dimension_semantics=("arbitrary",), vmem_limit_bytes=48 * 1024 * 1024),
        name="meta_proj",
    )(meta, g, w_bf16)


def _conv_block(win_ref, dww_ref, dwb_ref, y_ref, cb, nrows):
    base = CONV_HALO - (CONV_WIDTH - 1)
    cs = slice(cb * LANES, (cb + 1) * LANES)
    for rc in range(0, nrows, CONV_RC):
        total = None
        for phase in range(SUBLANES):
            acc = None
            for w in range(CONV_WIDTH):
                if (base + w) % SUBLANES != phase:
                    continue
                term = win_ref[rc + base + w:rc + base + w + CONV_RC, cs] * dww_ref[w:w + 1, cs]
                acc = term if acc is None else acc + term
            total = acc if total is None else total + acc
        y_ref[rc:rc + CONV_RC, cs] = total + dwb_ref[:, cs]


def _in_proj_kernel(tiles_per_batch, x_ref, g_ref, w_ref, gb_ref, ucm_ref, dww_ref, dwb_ref,
                    y_ref, q_ref, k_ref, v_ref, gate_ref, win_ref):
    tm = IN_TM
    i = pl.program_id(0)

    @pl.when(i % tiles_per_batch == 0)
    def _():
        win_ref[0:CONV_HALO - N_META, :] = jnp.zeros((CONV_HALO - N_META, C_CONV), F32)
        win_ref[CONV_HALO - N_META:CONV_HALO, :] = ucm_ref[...]

    @pl.when(i % tiles_per_batch != 0)
    def _():
        win_ref[0:CONV_HALO, :] = win_ref[tm:tm + CONV_HALO, :]

    u = _rms_norm(x_ref[...], g_ref[...]).astype(BF16)

    def glu_store(c, val):
        win_ref[CONV_HALO:, c:c + IN_NC] = val

    glu_jobs, rest = _proj_jobs(u, w_ref, gb_ref, glu_store, q_ref, k_ref, v_ref, gate_ref, IN_NC)
    for job in glu_jobs:
        job()
    conv_blocks = list(range(C_CONV // LANES))
    for job in rest:
        cb = conv_blocks.pop(0) if conv_blocks else None

        @pl.when(i >= 0)
        def _(job=job, cb=cb):
            job()
            if cb is not None:
                _conv_block(win_ref, dww_ref, dwb_ref, y_ref, cb, tm)
    assert not conv_blocks


def _in_proj(x2d, g, w_bf16, gate_b, uc_meta, dw_w, dw_b, seq):
    rows = x2d.shape[0]
    tm = IN_TM
    hp_spec = pl.BlockSpec((N_HEAD_PAIRS, tm, LANES), lambda i: (0, i, 0))
    hp_shape = jax.ShapeDtypeStruct((N_HEAD_PAIRS, rows, LANES), BF16)
    return pl.pallas_call(
        functools.partial(_in_proj_kernel, seq // tm),
        grid=(rows // tm,),
        in_specs=[
            pl.BlockSpec((tm, D_MODEL), lambda i: (i, 0)),
            _resident((1, D_MODEL)),
            _resident((D_MODEL, IN_W)),
            _resident((1, 2 * D_MODEL)),
            _resident((N_META, C_CONV)),
            _resident((CONV_WIDTH, C_CONV)),
            _resident((1, C_CONV)),
        ],
        out_specs=[
            pl.BlockSpec((tm, C_CONV), lambda i: (i, 0)),
            hp_spec, hp_spec, hp_spec,
            pl.BlockSpec((tm, 2 * D_MODEL), lambda i: (i, 0)),
        ],
        out_shape=[
            jax.ShapeDtypeStruct((rows, C_CONV), F32),
            hp_shape, hp_shape, hp_shape,
            jax.ShapeDtypeStruct((rows, 2 * D_MODEL), BF16),
        ],
        scratch_shapes=[pltpu.VMEM((CONV_HALO + tm, C_CONV), F32)],
        compiler_params=pltpu.CompilerParams(
            dimension_semantics=("arbitrary",), vmem_limit_bytes=48 * 1024 * 1024),
        name="in_proj",
    )(x2d, g, w_bf16, gate_b, uc_meta, dw_w, dw_b)


def _attn_kernel(q_ref, k_ref, v_ref, km_ref, vm_ref, o_ref, acc_ref, carry_ref):
    tq, tk = ATT_TQ, ATT_TK
    i = pl.program_id(2)
    chains = [(p, h) for p in range(ATT_NP) for h in range(2)]

    lane = lax.broadcasted_iota(jnp.int32, (tq, LANES), 1)
    in_head = (lane < HEAD_DIM, lane >= HEAD_DIM)
    qh = [jnp.where(in_head[h], q_ref[p], jnp.zeros((tq, LANES), BF16)) for p, h in chains]

    def tri(n):
        return (lax.broadcasted_iota(jnp.int32, (n, n), 0)
                >= lax.broadcasted_iota(jnp.int32, (n, n), 1)).astype(BF16)

    def softplus2(z, mask):
        sp = jnp.maximum(z, 0.0) + jnp.log2(1.0 + jnp.exp2(-jnp.abs(z)))
        if mask is not None:
            sp = jnp.where(mask, sp, 0.0)
        return sp.astype(BF16)

    def weights(z, r, mask):
        w = jnp.exp2(z - r)
        if mask is not None:
            w = jnp.where(mask, w, 0.0)
        return w.astype(BF16)

    def sweep(ks, vs, tri_kk, mask):
        nc = len(chains)
        z = [lax.dot_general(qh[c], ks[p], (((1,), (1,)), ((), ())), preferred_element_type=F32)
             for c, (p, h) in enumerate(chains)]
        sp = [softplus2(z[c], mask) for c in range(nc)]
        r = [jnp.dot(sp[c], tri_kk, preferred_element_type=F32) + carry_ref[c] for c in range(nc)]
        a = [weights(z[c], r[c], mask) for c in range(nc)]
        for c, (p, h) in enumerate(chains):
            acc_ref[c] += jnp.dot(a[c], vs[p], preferred_element_type=F32)
            carry_ref[c] = r[c][:, 0:1]

    def real_tile(j, mask):
        j0 = pl.multiple_of(j * tk, tk)
        sweep([k_ref[p, pl.ds(j0, tk), :] for p in range(ATT_NP)],
              [v_ref[p, pl.ds(j0, tk), :] for p in range(ATT_NP)], tri(tk), mask)

    def more_keys_matter():
        return (jnp.min(carry_ref[...]) < ATT_EXIT).astype(jnp.int32)

    acc_ref[...] = jnp.zeros_like(acc_ref)
    carry_ref[...] = jnp.zeros_like(carry_ref)

    real_tile(i, lax.broadcasted_iota(jnp.int32, (tq, tk), 1)
              < lax.broadcasted_iota(jnp.int32, (tq, tk), 0))

    def body(state):
        n, _ = state
        real_tile(i - 1 - n, None)
        return n + 1, more_keys_matter()

    _, go = lax.while_loop(lambda s: jnp.logical_and(s[0] < i, s[1] > 0), body,
                           (jnp.int32(0), more_keys_matter()))

    @pl.when(go > 0)
    def _():
        colm = lax.broadcasted_iota(jnp.int32, (tq, META_PAD), 1)
        sweep([km_ref[p] for p in range(ATT_NP)], [vm_ref[p] for p in range(ATT_NP)],
              tri(META_PAD), colm < N_META)

    for p in range(ATT_NP):
        o_ref[:, p * LANES:(p + 1) * LANES] = jnp.where(
            in_head[0], acc_ref[2 * p], acc_ref[2 * p + 1]).astype(o_ref.dtype)


def _attention(q, k, v, km, vm, batch, seq):
    nq = seq // ATT_TQ
    np_ = ATT_NP
    return pl.pallas_call(
        _attn_kernel,
        grid=(N_HEAD_PAIRS // np_, batch, nq),
        in_specs=[
            pl.BlockSpec((np_, ATT_TQ, LANES), lambda g, b, i: (g, b * nq + i, 0)),
            pl.BlockSpec((np_, seq, LANES), lambda g, b, i: (g, b, 0)),
            pl.BlockSpec((np_, seq, LANES), lambda g, b, i: (g, b, 0)),
            pl.BlockSpec((np_, META_PAD, LANES), lambda g, b, i: (g, 0, 0)),
            pl.BlockSpec((np_, META_PAD, LANES), lambda g, b, i: (g, 0, 0)),
        ],
        out_specs=pl.BlockSpec((ATT_TQ, np_ * LANES), lambda g, b, i: (b * nq + i, g)),
        out_shape=jax.ShapeDtypeStruct((batch * seq, ATTN_W), BF16),
        scratch_shapes=[
            pltpu.VMEM((2 * np_, ATT_TQ, LANES), F32),
            pltpu.VMEM((2 * np_, ATT_TQ, 1), F32),
        ],
        compiler_params=pltpu.CompilerParams(
            dimension_semantics=("arbitrary", "arbitrary", "arbitrary"),
            vmem_limit_bytes=40 * 1024 * 1024),
        name="sb_attn",
    )(q, k, v, km, vm)


def _post_kernel(y_ref, o_ref, gate_ref, x_ref, lng_ref, lnb_ref, wc_ref, wa_ref, wo_ref, pg_ref,
                 h1_ref):
    y_attn = jnp.dot(o_ref[...], wa_ref[...], preferred_element_type=F32)

    y = y_ref[...]
    mu = jnp.mean(y, axis=-1, keepdims=True)
    yc = y - mu
    var = jnp.mean(yc * yc, axis=-1, keepdims=True)
    n = yc * lax.rsqrt(var + LN_EPS) * lng_ref[...] + lnb_ref[...]
    c = (n * _sigmoid(n)).astype(BF16)

    y_conv = jnp.dot(c, wc_ref[...], preferred_element_type=F32)
    g_conv = gate_ref[:, :D_MODEL].astype(F32)
    g_attn = gate_ref[:, D_MODEL:].astype(F32)
    m = (g_conv * y_conv + g_attn * y_attn).astype(BF16)
    mix = jnp.dot(m, wo_ref[...], preferred_element_type=F32)
    h1_ref[...] = x_ref[...] + _rms_norm(mix, pg_ref[...])


def _post(y, o, gates, x2d, ln_g, ln_b, wc, wa, wo, pg):
    rows = x2d.shape[0]
    tm = POST_TM
    row_spec = lambda width: pl.BlockSpec((tm, width), lambda i: (i, 0))
    return pl.pallas_call(
        _post_kernel,
        grid=(rows // tm,),
        in_specs=[
            row_spec(C_CONV),
            row_spec(ATTN_W),
            row_spec(2 * D_MODEL),
            row_spec(D_MODEL),
            _resident((1, C_CONV)),
            _resident((1, C_CONV)),
            _resident((C_CONV, D_MODEL)),
            _resident((ATTN_W, D_MODEL)),
            _resident((D_MODEL, D_MODEL)),
            _resident((1, D_MODEL)),
        ],
        out_specs=row_spec(D_MODEL),
        out_shape=jax.ShapeDtypeStruct((rows, D_MODEL), F32),
        compiler_params=pltpu.CompilerParams(
            dimension_semantics=("arbitrary",), vmem_limit_bytes=40 * 1024 * 1024),
        name="post_mix",
    )(y, o, gates, x2d, ln_g, ln_b, wc, wa, wo, pg)


def _ffn_kernel(h_ref, g1_ref, wi_ref, wout_ref, g2_ref, out_ref, acc_ref):
    h = h_ref[...]
    u = _rms_norm(h, g1_ref[...]).astype(BF16)
    for n, c in enumerate(range(0, D_FF, FFN_NC)):
        a = jnp.dot(u, wi_ref[:, c:c + FFN_NC], preferred_element_type=F32)
        b = jnp.dot(u, wi_ref[:, D_FF + c:D_FF + c + FFN_NC], preferred_element_type=F32)
        s = (a * _sigmoid(a) * b).astype(BF16)
        part = jnp.dot(s, wout_ref[c:c + FFN_NC, :], preferred_element_type=F32)
        if n == 0:
            acc_ref[...] = part
        else:
            acc_ref[...] += part
    out_ref[...] = h + _rms_norm(acc_ref[...], g2_ref[...])


def _ffn(h1, g1, wi, wout, g2):
    rows = h1.shape[0]
    tm = FFN_TM
    return pl.pallas_call(
        _ffn_kernel,
        grid=(rows // tm,),
        in_specs=[
            pl.BlockSpec((tm, D_MODEL), lambda i: (i, 0)),
            _resident((1, D_MODEL)),
            _resident((D_MODEL, 2 * D_FF)),
            _resident((D_FF, D_MODEL)),
            _resident((1, D_MODEL)),
        ],
        out_specs=pl.BlockSpec((tm, D_MODEL), lambda i: (i, 0)),
        out_shape=jax.ShapeDtypeStruct((rows, D_MODEL), F32),
        scratch_shapes=[pltpu.VMEM((tm, D_MODEL), F32)],
        compiler_params=pltpu.CompilerParams(
            dimension_semantics=("arbitrary",), vmem_limit_bytes=52 * 1024 * 1024),
        name="ffn",
    )(h1, g1, wi, wout, g2)


def kernel(x, meta_tokens, pre_mix_g, w_in, gate_b, dw_w, dw_b, conv_ln_g, conv_ln_b,
           w_conv_out, w_attn_out, w_o, post_mix_g, pre_ffn_g, w_ffn_in, w_ffn_out,
           post_ffn_g):
    batch, seq, d = x.shape
    assert d == D_MODEL and seq % ATT_TQ == 0 and seq % IN_TM == 0
    assert w_in.shape[0] == 1, "single layer"
    rows = batch * seq
    assert rows % POST_TM == 0 and rows % FFN_TM == 0

    row = lambda p: p[0].reshape(1, -1)
    x2d = x.reshape(rows, d)
    w_in_b = w_in[0].astype(BF16)

    uc_m, k_m, v_m = _meta_proj(meta_tokens.astype(x.dtype), row(pre_mix_g), w_in_b)
    y, q, k, v, gates = _in_proj(x2d, row(pre_mix_g), w_in_b, row(gate_b), uc_m,
                                 dw_w[0], row(dw_b), seq)
    pad = ((0, 0), (0, META_PAD - N_META), (0, 0))
    o = _attention(q, k, v, jnp.pad(k_m, pad), jnp.pad(v_m, pad), batch, seq)

    h1 = _post(y, o, gates, x2d, row(conv_ln_g), row(conv_ln_b),
               w_conv_out[0].astype(BF16), w_attn_out[0].astype(BF16), w_o[0].astype(BF16),
               row(post_mix_g))
    out = _ffn(h1, row(pre_ffn_g), w_ffn_in[0].astype(BF16), w_ffn_out[0].astype(BF16),
               row(post_ffn_g))
    return out.reshape(batch, seq, d)
```

```python
import functools

import jax
import jax.numpy as jnp
from jax import lax
from jax.experimental import pallas as pl
from jax.experimental.pallas import tpu as pltpu

D_MODEL = 1024
N_META = 16
C_CONV = D_MODEL
CONV_WIDTH = 31
N_HEADS = 16
HEAD_DIM = 64
ATTN_W = N_HEADS * HEAD_DIM
D_FF = 2816
RMS_EPS = 1e-6
LN_EPS = 1e-5

LANES = 128
SUBLANES = 8
N_HEAD_PAIRS = ATTN_W // LANES
O_GLU = 0
O_Q = 2 * C_CONV
O_K = O_Q + ATTN_W
O_V = O_K + ATTN_W
O_GATE = O_V + ATTN_W
IN_W = O_GATE + 2 * D_MODEL

IN_TM = 512
IN_NC = 512
ATT_TQ = 256
ATT_TK = 256
ATT_NP = 8
META_PAD = 128
LOG2_E = 1.4426950408889634
ATT_EXIT = 151.0
POST_TM = 512
CONV_HALO = 32
CONV_RC = 64
FFN_TM = 512
FFN_NC = 256

F32 = jnp.float32
BF16 = jnp.bfloat16


def _sigmoid(x):
    return 1.0 / (1.0 + jnp.exp(-x))


def _rms_norm(x, g):
    return x * lax.rsqrt(jnp.mean(x * x, axis=-1, keepdims=True) + RMS_EPS) * g


def _resident(shape):
    nd = len(shape)
    return pl.BlockSpec(shape, lambda *_: (0,) * nd, pipeline_mode=pl.Buffered(1))


def _proj_jobs(u, w_ref, gb_ref, glu_store, q_ref, k_ref, v_ref, gate_ref, nc):
    def proj(off, c):
        return jnp.dot(u, w_ref[:, off + c:off + c + nc], preferred_element_type=F32)

    def glu(c):
        glu_store(c, proj(O_GLU, c) * _sigmoid(proj(O_GLU + C_CONV, c)))

    def heads(ref, off, scale, c):
        y = proj(off, c)
        if scale is not None:
            y = y * scale
        y = y.astype(BF16)
        for s in range(nc // LANES):
            ref[c // LANES + s] = y[:, s * LANES:(s + 1) * LANES]

    def gate(c):
        gate_ref[:, c:c + nc] = _sigmoid(proj(O_GATE, c) + gb_ref[:, c:c + nc]).astype(BF16)

    glu_jobs = [functools.partial(glu, c) for c in range(0, C_CONV, nc)]
    rest = []
    for ref, off, scale in ((q_ref, O_Q, LOG2_E * HEAD_DIM ** -0.5), (k_ref, O_K, None), (v_ref, O_V, None)):
        if ref is not None:
            rest += [functools.partial(heads, ref, off, scale, c) for c in range(0, ATTN_W, nc)]
    if gate_ref is not None:
        rest += [functools.partial(gate, c) for c in range(0, 2 * D_MODEL, nc)]
    return glu_jobs, rest


def _meta_proj_kernel(x_ref, g_ref, w_ref, uc_ref, k_ref, v_ref):
    u = _rms_norm(x_ref[...], g_ref[...]).astype(BF16)

    def glu_store(c, val):
        uc_ref[:, c:c + IN_NC] = val

    glu_jobs, rest = _proj_jobs(u, w_ref, None, glu_store, None, k_ref, v_ref, None, IN_NC)
    for job in glu_jobs + rest:
        job()


def _meta_proj(meta, g, w_bf16):
    hp_shape = jax.ShapeDtypeStruct((N_HEAD_PAIRS, N_META, LANES), BF16)
    return pl.pallas_call(
        _meta_proj_kernel,
        grid=(1,),
        in_specs=[_resident((N_META, D_MODEL)), _resident((1, D_MODEL)), _resident((D_MODEL, IN_W))],
        out_specs=[pl.BlockSpec((N_META, C_CONV), lambda i: (0, 0)),
                   pl.BlockSpec((N_HEAD_PAIRS, N_META, LANES), lambda i: (0, 0, 0)),
                   pl.BlockSpec((N_HEAD_PAIRS, N_META, LANES), lambda i: (0, 0, 0))],
        out_shape=[jax.ShapeDtypeStruct((N_META, C_CONV), F32), hp_shape, hp_shape],
        compiler_params=pltpu.CompilerParams(
            dimension_semantics=("arbitrary",), vmem_limit_bytes=48 * 1024 * 1024),
        name="meta_proj",
    )(meta, g, w_bf16)


def _conv_block(win_ref, dww_ref, dwb_ref, y_ref, cb, nrows):
    base = CONV_HALO - (CONV_WIDTH - 1)
    cs = slice(cb * LANES, (cb + 1) * LANES)
    for rc in range(0, nrows, CONV_RC):
        total = None
        for phase in range(SUBLANES):
            acc = None
            for w in range(CONV_WIDTH):
                if (base + w) % SUBLANES != phase:
                    continue
                term = win_ref[rc + base + w:rc + base + w + CONV_RC, cs] * dww_ref[w:w + 1, cs]
                acc = term if acc is None else acc + term
            total = acc if total is None else total + acc
        y_ref[rc:rc + CONV_RC, cs] = total + dwb_ref[:, cs]


def _in_proj_kernel(tiles_per_batch, x_ref, g_ref, w_ref, gb_ref, ucm_ref, dww_ref, dwb_ref,
                    y_ref, q_ref, k_ref, v_ref, gate_ref, win_ref):
    tm = IN_TM
    i = pl.program_id(0)

    @pl.when(i % tiles_per_batch == 0)
    def _():
        win_ref[0:CONV_HALO - N_META, :] = jnp.zeros((CONV_HALO - N_META, C_CONV), F32)
        win_ref[CONV_HALO - N_META:CONV_HALO, :] = ucm_ref[...]

    @pl.when(i % tiles_per_batch != 0)
    def _():
        win_ref[0:CONV_HALO, :] = win_ref[tm:tm + CONV_HALO, :]

    u = _rms_norm(x_ref[...], g_ref[...]).astype(BF16)

    def glu_store(c, val):
        win_ref[CONV_HALO:, c:c + IN_NC] = val

    glu_jobs, rest = _proj_jobs(u, w_ref, gb_ref, glu_store, q_ref, k_ref, v_ref, gate_ref, IN_NC)
    for job in glu_jobs:
        job()
    conv_blocks = list(range(C_CONV // LANES))
    for job in rest:
        cb = conv_blocks.pop(0) if conv_blocks else None

        @pl.when(i >= 0)
        def _(job=job, cb=cb):
            job()
            if cb is not None:
                _conv_block(win_ref, dww_ref, dwb_ref, y_ref, cb, tm)
    assert not conv_blocks


def _in_proj(x2d, g, w_bf16, gate_b, uc_meta, dw_w, dw_b, seq):
    rows = x2d.shape[0]
    tm = IN_TM
    hp_spec = pl.BlockSpec((N_HEAD_PAIRS, tm, LANES), lambda i: (0, i, 0))
    hp_shape = jax.ShapeDtypeStruct((N_HEAD_PAIRS, rows, LANES), BF16)
    return pl.pallas_call(
        functools.partial(_in_proj_kernel, seq // tm),
        grid=(rows // tm,),
        in_specs=[
            pl.BlockSpec((tm, D_MODEL), lambda i: (i, 0)),
            _resident((1, D_MODEL)),
            _resident((D_MODEL, IN_W)),
            _resident((1, 2 * D_MODEL)),
            _resident((N_META, C_CONV)),
            _resident((CONV_WIDTH, C_CONV)),
            _resident((1, C_CONV)),
        ],
        out_specs=[
            pl.BlockSpec((tm, C_CONV), lambda i: (i, 0)),
            hp_spec, hp_spec, hp_spec,
            pl.BlockSpec((tm, 2 * D_MODEL), lambda i: (i, 0)),
        ],
        out_shape=[
            jax.ShapeDtypeStruct((rows, C_CONV), F32),
            hp_shape, hp_shape, hp_shape,
            jax.ShapeDtypeStruct((rows, 2 * D_MODEL), BF16),
        ],
        scratch_shapes=[pltpu.VMEM((CONV_HALO + tm, C_CONV), F32)],
        compiler_params=pltpu.CompilerParams(
            dimension_semantics=("arbitrary",), vmem_limit_bytes=48 * 1024 * 1024),
        name="in_proj",
    )(x2d, g, w_bf16, gate_b, uc_meta, dw_w, dw_b)


def _attn_kernel(q_ref, k_ref, v_ref, km_ref, vm_ref, o_ref, acc_ref, carry_ref):
    tq, tk = ATT_TQ, ATT_TK
    i = pl.program_id(2)
    chains = [(p, h) for p in range(ATT_NP) for h in range(2)]

    lane = lax.broadcasted_iota(jnp.int32, (tq, LANES), 1)
    in_head = (lane < HEAD_DIM, lane >= HEAD_DIM)
    qh = [jnp.where(in_head[h], q_ref[p], jnp.zeros((tq, LANES), BF16)) for p, h in chains]

    def tri(n):
        return (lax.broadcasted_iota(jnp.int32, (n, n), 0)
                >= lax.broadcasted_iota(jnp.int32, (n, n), 1)).astype(BF16)

    def softplus2(z, mask):
        sp = jnp.maximum(z, 0.0) + jnp.log2(1.0 + jnp.exp2(-jnp.abs(z)))
        if mask is not None:
            sp = jnp.where(mask, sp, 0.0)
        return sp.astype(BF16)

    def weights(z, r, mask):
        w = jnp.exp2(z - r)
        if mask is not None:
            w = jnp.where(mask, w, 0.0)
        return w.astype(BF16)

    def sweep(tiles):
        nc = len(chains)
        z = [[lax.dot_general(qh[c], ks[p], (((1,), (1,)), ((), ())), preferred_element_type=F32)
              for c, (p, h) in enumerate(chains)] for ks, _, _, _ in tiles]
        sp = [[softplus2(z[t][c], mask) for c in range(nc)]
              for t, (_, _, _, mask) in enumerate(tiles)]
        carry = [carry_ref[c] for c in range(nc)]
        r = []
        for t, (_, _, tri_kk, _) in enumerate(tiles):
            r.append([jnp.dot(sp[t][c], tri_kk, preferred_element_type=F32) + carry[c]
                      for c in range(nc)])
            carry = [r[t][c][:, 0:1] for c in range(nc)]
        a = [[weights(z[t][c], r[t][c], mask) for c in range(nc)]
             for t, (_, _, _, mask) in enumerate(tiles)]
        for c, (p, h) in enumerate(chains):
            pv = None
            for t, (_, vs, _, _) in enumerate(tiles):
                d = jnp.dot(a[t][c], vs[p], preferred_element_type=F32)
                pv = d if pv is None else pv + d
            acc_ref[c] += pv
            carry_ref[c] = carry[c]

    def real_tile(j, mask):
        j0 = pl.multiple_of(j * tk, tk)
        return ([k_ref[p, pl.ds(j0, tk), :] for p in range(ATT_NP)],
                [v_ref[p, pl.ds(j0, tk), :] for p in range(ATT_NP)], tri(tk), mask)

    def meta_tile():
        colm = lax.broadcasted_iota(jnp.int32, (tq, META_PAD), 1)
        return ([km_ref[p] for p in range(ATT_NP)], [vm_ref[p] for p in range(ATT_NP)],
                tri(META_PAD), colm < N_META)

    def more_keys_matter():
        return (jnp.min(carry_ref[...]) < ATT_EXIT).astype(jnp.int32)

    acc_ref[...] = jnp.zeros_like(acc_ref)
    carry_ref[...] = jnp.zeros_like(carry_ref)

    diag_mask = (lax.broadcasted_iota(jnp.int32, (tq, tk), 1)
                 < lax.broadcasted_iota(jnp.int32, (tq, tk), 0))

    @pl.when(i == 0)
    def _():
        sweep([real_tile(i, diag_mask), meta_tile()])

    @pl.when(i > 0)
    def _():
        sweep([real_tile(i, diag_mask), real_tile(i - 1, None)])

        def body(state):
            n, _ = state
            sweep([real_tile(i - 1 - n, None)])
            return n + 1, more_keys_matter()

        _, go = lax.while_loop(lambda s: jnp.logical_and(s[0] < i, s[1] > 0), body,
                               (jnp.int32(1), more_keys_matter()))

        @pl.when(go > 0)
        def _():
            sweep([meta_tile()])

    for p in range(ATT_NP):
        o_ref[:, p * LANES:(p + 1) * LANES] = jnp.where(
            in_head[0], acc_ref[2 * p], acc_ref[2 * p + 1]).astype(o_ref.dtype)


def _attention(q, k, v, km, vm, batch, seq):
    nq = seq // ATT_TQ
    np_ = ATT_NP
    return pl.pallas_call(
        _attn_kernel,
        grid=(N_HEAD_PAIRS // np_, batch, nq),
        in_specs=[
            pl.BlockSpec((np_, ATT_TQ, LANES), lambda g, b, i: (g, b * nq + i, 0)),
            pl.BlockSpec((np_, seq, LANES), lambda g, b, i: (g, b, 0)),
            pl.BlockSpec((np_, seq, LANES), lambda g, b, i: (g, b, 0)),
            pl.BlockSpec((np_, META_PAD, LANES), lambda g, b, i: (g, 0, 0)),
            pl.BlockSpec((np_, META_PAD, LANES), lambda g, b, i: (g, 0, 0)),
        ],
        out_specs=pl.BlockSpec((ATT_TQ, np_ * LANES), lambda g, b, i: (b * nq + i, g)),
        out_shape=jax.ShapeDtypeStruct((batch * seq, ATTN_W), BF16),
        scratch_shapes=[
            pltpu.VMEM((2 * np_, ATT_TQ, LANES), F32),
            pltpu.VMEM((2 * np_, ATT_TQ, 1), F32),
        ],
        compiler_params=pltpu.CompilerParams(
            dimension_semantics=("arbitrary", "arbitrary", "arbitrary"),
            vmem_limit_bytes=58 * 1024 * 1024),
        name="sb_attn",
    )(q, k, v, km, vm)


def _post_kernel(y_ref, o_ref, gate_ref, x_ref, lng_ref, lnb_ref, wc_ref, wa_ref, wo_ref, pg_ref,
                 h1_ref):
    y_attn = jnp.dot(o_ref[...], wa_ref[...], preferred_element_type=F32)

    y = y_ref[...]
    mu = jnp.mean(y, axis=-1, keepdims=True)
    yc = y - mu
    var = jnp.mean(yc * yc, axis=-1, keepdims=True)
    n = yc * lax.rsqrt(var + LN_EPS) * lng_ref[...] + lnb_ref[...]
    c = (n * _sigmoid(n)).astype(BF16)

    y_conv = jnp.dot(c, wc_ref[...], preferred_element_type=F32)
    g_conv = gate_ref[:, :D_MODEL].astype(F32)
    g_attn = gate_ref[:, D_MODEL:].astype(F32)
    m = (g_conv * y_conv + g_attn * y_attn).astype(BF16)
    mix = jnp.dot(m, wo_ref[...], preferred_element_type=F32)
    h1_ref[...] = x_ref[...] + _rms_norm(mix, pg_ref[...])


def _post(y, o, gates, x2d, ln_g, ln_b, wc, wa, wo, pg):
    rows = x2d.shape[0]
    tm = POST_TM
    row_spec = lambda width: pl.BlockSpec((tm, width), lambda i: (i, 0))
    return pl.pallas_call(
        _post_kernel,
        grid=(rows // tm,),
        in_specs=[
            row_spec(C_CONV),
            row_spec(ATTN_W),
            row_spec(2 * D_MODEL),
            row_spec(D_MODEL),
            _resident((1, C_CONV)),
            _resident((1, C_CONV)),
            _resident((C_CONV, D_MODEL)),
            _resident((ATTN_W, D_MODEL)),
            _resident((D_MODEL, D_MODEL)),
            _resident((1, D_MODEL)),
        ],
        out_specs=row_spec(D_MODEL),
        out_shape=jax.ShapeDtypeStruct((rows, D_MODEL), F32),
        compiler_params=pltpu.CompilerParams(
            dimension_semantics=("arbitrary",), vmem_limit_bytes=40 * 1024 * 1024),
        name="post_mix",
    )(y, o, gates, x2d, ln_g, ln_b, wc, wa, wo, pg)


def _ffn_kernel(h_ref, g1_ref, wi_ref, wout_ref, g2_ref, out_ref, acc_ref):
    h = h_ref[...]
    u = _rms_norm(h, g1_ref[...]).astype(BF16)
    for n, c in enumerate(range(0, D_FF, FFN_NC)):
        wa = wi_ref[:, c:c + FFN_NC].astype(BF16)
        wb = wi_ref[:, D_FF + c:D_FF + c + FFN_NC].astype(BF16)
        a = jnp.dot(u, wa, preferred_element_type=F32)
        b = jnp.dot(u, wb, preferred_element_type=F32)
        s = (a * _sigmoid(a) * b).astype(BF16)
        part = jnp.dot(s, wout_ref[c:c + FFN_NC, :].astype(BF16), preferred_element_type=F32)
        if n == 0:
            acc_ref[...] = part
        else:
            acc_ref[...] += part
    out_ref[...] = h + _rms_norm(acc_ref[...], g2_ref[...])


def _ffn(h1, g1, wi, wout, g2):
    rows = h1.shape[0]
    tm = FFN_TM
    return pl.pallas_call(
        _ffn_kernel,
        grid=(rows // tm,),
        in_specs=[
            pl.BlockSpec((tm, D_MODEL), lambda i: (i, 0)),
            _resident((1, D_MODEL)),
            _resident((D_MODEL, 2 * D_FF)),
            _resident((D_FF, D_MODEL)),
            _resident((1, D_MODEL)),
        ],
        out_specs=pl.BlockSpec((tm, D_MODEL), lambda i: (i, 0)),
        out_shape=jax.ShapeDtypeStruct((rows, D_MODEL), F32),
        scratch_shapes=[pltpu.VMEM((tm, D_MODEL), F32)],
        compiler_params=pltpu.CompilerParams(
            dimension_semantics=("arbitrary",), vmem_limit_bytes=58 * 1024 * 1024),
        name="ffn",
    )(h1, g1, wi, wout, g2)


def kernel(x, meta_tokens, pre_mix_g, w_in, gate_b, dw_w, dw_b, conv_ln_g, conv_ln_b,
           w_conv_out, w_attn_out, w_o, post_mix_g, pre_ffn_g, w_ffn_in, w_ffn_out,
           post_ffn_g):
    batch, seq, d = x.shape
    assert d == D_MODEL and seq % ATT_TQ == 0 and seq % IN_TM == 0
    assert w_in.shape[0] == 1, "single layer"
    rows = batch * seq
    assert rows % POST_TM == 0 and rows % FFN_TM == 0

    row = lambda p: p[0].reshape(1, -1)
    x2d = x.reshape(rows, d)
    w_in_b = w_in[0].astype(BF16)

    uc_m, k_m, v_m = _meta_proj(meta_tokens.astype(x.dtype), row(pre_mix_g), w_in_b)
    y, q, k, v, gates = _in_proj(x2d, row(pre_mix_g), w_in_b, row(gate_b), uc_m,
                                 dw_w[0], row(dw_b), seq)
    pad = ((0, 0), (0, META_PAD - N_META), (0, 0))
    o = _attention(q, k, v, jnp.pad(k_m, pad), jnp.pad(v_m, pad), batch, seq)

    h1 = _post(y, o, gates, x2d, row(conv_ln_g), row(conv_ln_b),
               w_conv_out[0].astype(BF16), w_attn_out[0].astype(BF16), w_o[0].astype(BF16),
               row(post_mix_g))
    out = _ffn(h1, row(pre_ffn_g), w_ffn_in[0], w_ffn_out[0], row(post_ffn_g))
    return out.reshape(batch, seq, d)
```

```python
import functools

import jax
import jax.numpy as jnp
from jax import lax
from jax.experimental import pallas as pl
from jax.experimental.pallas import tpu as pltpu

D_MODEL = 1024
N_META = 16
C_CONV = D_MODEL
CONV_WIDTH = 31
N_HEADS = 16
HEAD_DIM = 64
ATTN_W = N_HEADS * HEAD_DIM
D_FF = 2816
RMS_EPS = 1e-6
LN_EPS = 1e-5

LANES = 128
SUBLANES = 8
N_HEAD_PAIRS = ATTN_W // LANES
O_GLU = 0
O_Q = 2 * C_CONV
O_K = O_Q + ATTN_W
O_V = O_K + ATTN_W
O_GATE = O_V + ATTN_W
IN_W = O_GATE + 2 * D_MODEL

IN_TM = 512
IN_NC = 512
ATT_TQ = 256
ATT_TK = 256
ATT_NP = 8
META_PAD = 128
LOG2_E = 1.4426950408889634
ATT_EXIT = 151.0
POST_TM = 512
CONV_HALO = 32
CONV_RC = 64
FFN_TM = 512
FFN_NC = 256

F32 = jnp.float32
BF16 = jnp.bfloat16


def _sigmoid(x):
    return 1.0 / (1.0 + jnp.exp(-x))


def _rms_norm(x, g):
    return x * lax.rsqrt(jnp.mean(x * x, axis=-1, keepdims=True) + RMS_EPS) * g


def _resident(shape):
    nd = len(shape)
    return pl.BlockSpec(shape, lambda *_: (0,) * nd, pipeline_mode=pl.Buffered(1))


def _proj_jobs(u, w_ref, gb_ref, glu_store, q_ref, k_ref, v_ref, gate_ref, nc):
    def proj(off, c):
        return jnp.dot(u, w_ref[:, off + c:off + c + nc], preferred_element_type=F32)

    def glu(c):
        glu_store(c, proj(O_GLU, c) * _sigmoid(proj(O_GLU + C_CONV, c)))

    def heads(ref, off, scale, c):
        y = proj(off, c)
        if scale is not None:
            y = y * scale
        y = y.astype(BF16)
        for s in range(nc // LANES):
            ref[c // LANES + s] = y[:, s * LANES:(s + 1) * LANES]

    def gate(c):
        gate_ref[:, c:c + nc] = _sigmoid(proj(O_GATE, c) + gb_ref[:, c:c + nc]).astype(BF16)

    glu_jobs = [functools.partial(glu, c) for c in range(0, C_CONV, nc)]
    rest = []
    for ref, off, scale in ((q_ref, O_Q, LOG2_E * HEAD_DIM ** -0.5), (k_ref, O_K, None), (v_ref, O_V, None)):
        if ref is not None:
            rest += [functools.partial(heads, ref, off, scale, c) for c in range(0, ATTN_W, nc)]
    if gate_ref is not None:
        rest += [functools.partial(gate, c) for c in range(0, 2 * D_MODEL, nc)]
    return glu_jobs, rest


def _meta_proj_kernel(x_ref, g_ref, w_ref, uc_ref, k_ref, v_ref):
    u = _rms_norm(x_ref[...], g_ref[...]).astype(BF16)

    def glu_store(c, val):
        uc_ref[:, c:c + IN_NC] = val

    glu_jobs, rest = _proj_jobs(u, w_ref, None, glu_store, None, k_ref, v_ref, None, IN_NC)
    for job in glu_jobs + rest:
        job()


def _meta_proj(meta, g, w_bf16):
    hp_shape = jax.ShapeDtypeStruct((N_HEAD_PAIRS, N_META, LANES), BF16)
    return pl.pallas_call(
        _meta_proj_kernel,
        grid=(1,),
        in_specs=[_resident((N_META, D_MODEL)), _resident((1, D_MODEL)), _resident((D_MODEL, IN_W))],
        out_specs=[pl.BlockSpec((N_META, C_CONV), lambda i: (0, 0)),
                   pl.BlockSpec((N_HEAD_PAIRS, N_META, LANES), lambda i: (0, 0, 0)),
                   pl.BlockSpec((N_HEAD_PAIRS, N_META, LANES), lambda i: (0, 0, 0))],
        out_shape=[jax.ShapeDtypeStruct((N_META, C_CONV), F32), hp_shape, hp_shape],
        compiler_params=pltpu.CompilerParams(
            dimension_semantics=("arbitrary",), vmem_limit_bytes=48 * 1024 * 1024),
        name="meta_proj",
    )(meta, g, w_bf16)


def _conv_block(win_ref, dww_ref, dwb_ref, y_ref, cb, nrows):
    base = CONV_HALO - (CONV_WIDTH - 1)
    cs = slice(cb * LANES, (cb + 1) * LANES)
    for rc in range(0, nrows, CONV_RC):
        total = None
        for phase in range(SUBLANES):
            acc = None
            for w in range(CONV_WIDTH):
                if (base + w) % SUBLANES != phase:
                    continue
                term = win_ref[rc + base + w:rc + base + w + CONV_RC, cs] * dww_ref[w:w + 1, cs]
                acc = term if acc is None else acc + term
            total = acc if total is None else total + acc
        y_ref[rc:rc + CONV_RC, cs] = total + dwb_ref[:, cs]


def _in_proj_kernel(tiles_per_batch, x_ref, g_ref, w_ref, gb_ref, ucm_ref, dww_ref, dwb_ref,
                    y_ref, q_ref, k_ref, v_ref, gate_ref, win_ref):
    tm = IN_TM
    i = pl.program_id(0)

    @pl.when(i % tiles_per_batch == 0)
    def _():
        win_ref[0:CONV_HALO - N_META, :] = jnp.zeros((CONV_HALO - N_META, C_CONV), F32)
        win_ref[CONV_HALO - N_META:CONV_HALO, :] = ucm_ref[...]

    @pl.when(i % tiles_per_batch != 0)
    def _():
        win_ref[0:CONV_HALO, :] = win_ref[tm:tm + CONV_HALO, :]

    u = _rms_norm(x_ref[...], g_ref[...]).astype(BF16)

    def glu_store(c, val):
        win_ref[CONV_HALO:, c:c + IN_NC] = val

    glu_jobs, rest = _proj_jobs(u, w_ref, gb_ref, glu_store, q_ref, k_ref, v_ref, gate_ref, IN_NC)
    for job in glu_jobs:
        job()
    conv_blocks = list(range(C_CONV // LANES))
    for job in rest:
        cb = conv_blocks.pop(0) if conv_blocks else None

        @pl.when(i >= 0)
        def _(job=job, cb=cb):
            job()
            if cb is not None:
                _conv_block(win_ref, dww_ref, dwb_ref, y_ref, cb, tm)
    assert not conv_blocks


def _in_proj(x2d, g, w_bf16, gate_b, uc_meta, dw_w, dw_b, seq):
    rows = x2d.shape[0]
    tm = IN_TM
    hp_spec = pl.BlockSpec((N_HEAD_PAIRS, tm, LANES), lambda i: (0, i, 0))
    hp_shape = jax.ShapeDtypeStruct((N_HEAD_PAIRS, rows, LANES), BF16)
    return pl.pallas_call(
        functools.partial(_in_proj_kernel, seq // tm),
        grid=(rows // tm,),
        in_specs=[
            pl.BlockSpec((tm, D_MODEL), lambda i: (i, 0)),
            _resident((1, D_MODEL)),
            _resident((D_MODEL, IN_W)),
            _resident((1, 2 * D_MODEL)),
            _resident((N_META, C_CONV)),
            _resident((CONV_WIDTH, C_CONV)),
            _resident((1, C_CONV)),
        ],
        out_specs=[
            pl.BlockSpec((tm, C_CONV), lambda i: (i, 0)),
            hp_spec, hp_spec, hp_spec,
            pl.BlockSpec((tm, 2 * D_MODEL), lambda i: (i, 0)),
        ],
        out_shape=[
            jax.ShapeDtypeStruct((rows, C_CONV), F32),
            hp_shape, hp_shape, hp_shape,
            jax.ShapeDtypeStruct((rows, 2 * D_MODEL), BF16),
        ],
        scratch_shapes=[pltpu.VMEM((CONV_HALO + tm, C_CONV), F32)],
        compiler_params=pltpu.CompilerParams(
            dimension_semantics=("arbitrary",), vmem_limit_bytes=48 * 1024 * 1024),
        name="in_proj",
    )(x2d, g, w_bf16, gate_b, uc_meta, dw_w, dw_b)


def _attn_kernel(q_ref, k_ref, v_ref, km_ref, vm_ref, o_ref, acc_ref, carry_ref):
    tq, tk = ATT_TQ, ATT_TK
    i = pl.program_id(2)
    chains = [(p, h) for p in range(ATT_NP) for h in range(2)]

    lane = lax.broadcasted_iota(jnp.int32, (tq, LANES), 1)
    in_head = (lane < HEAD_DIM, lane >= HEAD_DIM)
    qh = [jnp.where(in_head[h], q_ref[p], jnp.zeros((tq, LANES), BF16)) for p, h in chains]

    def tri(n):
        return (lax.broadcasted_iota(jnp.int32, (n, n), 0)
                >= lax.broadcasted_iota(jnp.int32, (n, n), 1)).astype(BF16)

    def softplus2(z, mask):
        sp = jnp.maximum(z, 0.0) + jnp.log2(1.0 + jnp.exp2(-jnp.abs(z)))
        if mask is not None:
            sp = jnp.where(mask, sp, 0.0)
        return sp.astype(BF16)

    def weights(z, r, mask):
        w = jnp.exp2(z - r)
        if mask is not None:
            w = jnp.where(mask, w, 0.0)
        return w.astype(BF16)

    def sweep(tiles, first=False):
        nc = len(chains)
        z = [[lax.dot_general(qh[c], ks[p], (((1,), (1,)), ((), ())), preferred_element_type=F32)
              for c, (p, h) in enumerate(chains)] for ks, _, _, _ in tiles]
        sp = [[softplus2(z[t][c], mask) for c in range(nc)]
              for t, (_, _, _, mask) in enumerate(tiles)]
        carry = None if first else [carry_ref[c] for c in range(nc)]
        r = []
        for t, (_, _, tri_kk, _) in enumerate(tiles):
            local = [jnp.dot(sp[t][c], tri_kk, preferred_element_type=F32) for c in range(nc)]
            r.append(local if carry is None else [local[c] + carry[c] for c in range(nc)])
            carry = [r[t][c][:, 0:1] for c in range(nc)]
        a = [[weights(z[t][c], r[t][c], mask) for c in range(nc)]
             for t, (_, _, _, mask) in enumerate(tiles)]
        for c, (p, h) in enumerate(chains):
            pv = None
            for t, (_, vs, _, _) in enumerate(tiles):
                d = jnp.dot(a[t][c], vs[p], preferred_element_type=F32)
                pv = d if pv is None else pv + d
            if first:
                acc_ref[c] = pv
            else:
                acc_ref[c] += pv
            carry_ref[c] = carry[c]

    def real_tile(j, mask):
        j0 = pl.multiple_of(j * tk, tk)
        return ([k_ref[p, pl.ds(j0, tk), :] for p in range(ATT_NP)],
                [v_ref[p, pl.ds(j0, tk), :] for p in range(ATT_NP)], tri(tk), mask)

    def meta_tile():
        colm = lax.broadcasted_iota(jnp.int32, (tq, META_PAD), 1)
        return ([km_ref[p] for p in range(ATT_NP)], [vm_ref[p] for p in range(ATT_NP)],
                tri(META_PAD), colm < N_META)

    def more_keys_matter():
        return (jnp.min(carry_ref[...]) < ATT_EXIT).astype(jnp.int32)

    diag_mask = (lax.broadcasted_iota(jnp.int32, (tq, tk), 1)
                 < lax.broadcasted_iota(jnp.int32, (tq, tk), 0))

    @pl.when(i == 0)
    def _():
        sweep([real_tile(i, diag_mask), meta_tile()], first=True)

    @pl.when(i > 0)
    def _():
        sweep([real_tile(i, diag_mask), real_tile(i - 1, None)], first=True)

        def body(state):
            n, _ = state
            sweep([real_tile(i - 1 - n, None)])
            return n + 1, more_keys_matter()

        _, go = lax.while_loop(lambda s: jnp.logical_and(s[0] < i, s[1] > 0), body,
                               (jnp.int32(1), more_keys_matter()))

        @pl.when(go > 0)
        def _():
            sweep([meta_tile()])

    for p in range(ATT_NP):
        o_ref[:, p * LANES:(p + 1) * LANES] = jnp.where(
            in_head[0], acc_ref[2 * p], acc_ref[2 * p + 1]).astype(o_ref.dtype)


def _attention(q, k, v, km, vm, batch, seq):
    nq = seq // ATT_TQ
    np_ = ATT_NP
    return pl.pallas_call(
        _attn_kernel,
        grid=(N_HEAD_PAIRS // np_, batch, nq),
        in_specs=[
            pl.BlockSpec((np_, ATT_TQ, LANES), lambda g, b, i: (g, b * nq + i, 0)),
            pl.BlockSpec((np_, seq, LANES), lambda g, b, i: (g, b, 0)),
            pl.BlockSpec((np_, seq, LANES), lambda g, b, i: (g, b, 0)),
            pl.BlockSpec((np_, META_PAD, LANES), lambda g, b, i: (g, 0, 0)),
            pl.BlockSpec((np_, META_PAD, LANES), lambda g, b, i: (g, 0, 0)),
        ],
        out_specs=pl.BlockSpec((ATT_TQ, np_ * LANES), lambda g, b, i: (b * nq + i, g)),
        out_shape=jax.ShapeDtypeStruct((batch * seq, ATTN_W), BF16),
        scratch_shapes=[
            pltpu.VMEM((2 * np_, ATT_TQ, LANES), F32),
            pltpu.VMEM((2 * np_, ATT_TQ, 1), F32),
        ],
        compiler_params=pltpu.CompilerParams(
            dimension_semantics=("arbitrary", "arbitrary", "arbitrary"),
            vmem_limit_bytes=58 * 1024 * 1024),
        name="sb_attn",
    )(q, k, v, km, vm)


def _post_kernel(y_ref, o_ref, gate_ref, x_ref, lng_ref, lnb_ref, wc_ref, wa_ref, wo_ref, pg_ref,
                 h1_ref):
    y_attn = jnp.dot(o_ref[...], wa_ref[...], preferred_element_type=F32)

    y = y_ref[...]
    mu = jnp.mean(y, axis=-1, keepdims=True)
    yc = y - mu
    var = jnp.mean(yc * yc, axis=-1, keepdims=True)
    n = yc * lax.rsqrt(var + LN_EPS) * lng_ref[...] + lnb_ref[...]
    c = (n * _sigmoid(n)).astype(BF16)

    y_conv = jnp.dot(c, wc_ref[...], preferred_element_type=F32)
    g_conv = gate_ref[:, :D_MODEL].astype(F32)
    g_attn = gate_ref[:, D_MODEL:].astype(F32)
    m = (g_conv * y_conv + g_attn * y_attn).astype(BF16)
    mix = jnp.dot(m, wo_ref[...], preferred_element_type=F32)
    h1_ref[...] = x_ref[...] + _rms_norm(mix, pg_ref[...])


def _post(y, o, gates, x2d, ln_g, ln_b, wc, wa, wo, pg):
    rows = x2d.shape[0]
    tm = POST_TM
    row_spec = lambda width: pl.BlockSpec((tm, width), lambda i: (i, 0))
    return pl.pallas_call(
        _post_kernel,
        grid=(rows // tm,),
        in_specs=[
            row_spec(C_CONV),
            row_spec(ATTN_W),
            row_spec(2 * D_MODEL),
            row_spec(D_MODEL),
            _resident((1, C_CONV)),
            _resident((1, C_CONV)),
            _resident((C_CONV, D_MODEL)),
            _resident((ATTN_W, D_MODEL)),
            _resident((D_MODEL, D_MODEL)),
            _resident((1, D_MODEL)),
        ],
        out_specs=row_spec(D_MODEL),
        out_shape=jax.ShapeDtypeStruct((rows, D_MODEL), F32),
        compiler_params=pltpu.CompilerParams(
            dimension_semantics=("arbitrary",), vmem_limit_bytes=40 * 1024 * 1024),
        name="post_mix",
    )(y, o, gates, x2d, ln_g, ln_b, wc, wa, wo, pg)


def _ffn_kernel(h_ref, g1_ref, wi_ref, wout_ref, g2_ref, out_ref, acc_ref):
    h = h_ref[...]
    u = _rms_norm(h, g1_ref[...]).astype(BF16)
    for n, c in enumerate(range(0, D_FF, FFN_NC)):
        wa = wi_ref[:, c:c + FFN_NC].astype(BF16)
        wb = wi_ref[:, D_FF + c:D_FF + c + FFN_NC].astype(BF16)
        a = jnp.dot(u, wa, preferred_element_type=F32)
        b = jnp.dot(u, wb, preferred_element_type=F32)
        s = (a * _sigmoid(a) * b).astype(BF16)
        part = jnp.dot(s, wout_ref[c:c + FFN_NC, :].astype(BF16), preferred_element_type=F32)
        if n == 0:
            acc_ref[...] = part
        else:
            acc_ref[...] += part
    out_ref[...] = h + _rms_norm(acc_ref[...], g2_ref[...])


def _ffn(h1, g1, wi, wout, g2):
    rows = h1.shape[0]
    tm = FFN_TM
    return pl.pallas_call(
        _ffn_kernel,
        grid=(rows // tm,),
        in_specs=[
            pl.BlockSpec((tm, D_MODEL), lambda i: (i, 0)),
            _resident((1, D_MODEL)),
            _resident((D_MODEL, 2 * D_FF)),
            _resident((D_FF, D_MODEL)),
            _resident((1, D_MODEL)),
        ],
        out_specs=pl.BlockSpec((tm, D_MODEL), lambda i: (i, 0)),
        out_shape=jax.ShapeDtypeStruct((rows, D_MODEL), F32),
        scratch_shapes=[pltpu.VMEM((tm, D_MODEL), F32)],
        compiler_params=pltpu.CompilerParams(
            dimension_semantics=("arbitrary",), vmem_limit_bytes=58 * 1024 * 1024),
        name="ffn",
    )(h1, g1, wi, wout, g2)


def kernel(x, meta_tokens, pre_mix_g, w_in, gate_b, dw_w, dw_b, conv_ln_g, conv_ln_b,
           w_conv_out, w_attn_out, w_o, post_mix_g, pre_ffn_g, w_ffn_in, w_ffn_out,
           post_ffn_g):
    batch, seq, d = x.shape
    assert d == D_MODEL and seq % ATT_TQ == 0 and seq % IN_TM == 0
    assert w_in.shape[0] == 1, "single layer"
    rows = batch * seq
    assert rows % POST_TM == 0 and rows % FFN_TM == 0

    row = lambda p: p[0].reshape(1, -1)
    x2d = x.reshape(rows, d)
    w_in_b = w_in[0].astype(BF16)

    uc_m, k_m, v_m = _meta_proj(meta_tokens.astype(x.dtype), row(pre_mix_g), w_in_b)
    y, q, k, v, gates = _in_proj(x2d, row(pre_mix_g), w_in_b, row(gate_b), uc_m,
                                 dw_w[0], row(dw_b), seq)
    pad = ((0, 0), (0, META_PAD - N_META), (0, 0))
    o = _attention(q, k, v, jnp.pad(k_m, pad), jnp.pad(v_m, pad), batch, seq)

    h1 = _post(y, o, gates, x2d, row(conv_ln_g), row(conv_ln_b),
               w_conv_out[0].astype(BF16), w_attn_out[0].astype(BF16), w_o[0].astype(BF16),
               row(post_mix_g))
    out = _ffn(h1, row(pre_ffn_g), w_ffn_in[0], w_ffn_out[0], row(post_ffn_g))
    return out.reshape(batch, seq, d)
```

```python
import functools

import jax
import jax.numpy as jnp
from jax import lax
from jax.experimental import pallas as pl
from jax.experimental.pallas import tpu as pltpu

D_MODEL = 1024
N_META = 16
C_CONV = D_MODEL
CONV_WIDTH = 31
N_HEADS = 16
HEAD_DIM = 64
ATTN_W = N_HEADS * HEAD_DIM
D_FF = 2816
RMS_EPS = 1e-6
LN_EPS = 1e-5

LANES = 128
SUBLANES = 8
N_HEAD_PAIRS = ATTN_W // LANES
O_GLU = 0
O_Q = 2 * C_CONV
O_K = O_Q + ATTN_W
O_V = O_K + ATTN_W
O_GATE = O_V + ATTN_W
IN_W = O_GATE + 2 * D_MODEL

META_NC = C_CONV
assert META_NC == ATTN_W and IN_W % META_NC == 0
IN_TM = 512
IN_NC = 512
ATT_TQ = 256
ATT_TK = 256
ATT_NP = 8
META_PAD = 128
LOG2_E = 1.4426950408889634
ATT_EXIT = 151.0
POST_TM = 512
CONV_HALO = 32
CONV_RC = 64
FFN_TM = 512
FFN_NC = 256

F32 = jnp.float32
BF16 = jnp.bfloat16


def _sigmoid(x):
    return 1.0 / (1.0 + jnp.exp(-x))


def _rms_norm(x, g):
    return x * lax.rsqrt(jnp.mean(x * x, axis=-1, keepdims=True) + RMS_EPS) * g


def _resident(shape):
    nd = len(shape)
    return pl.BlockSpec(shape, lambda *_: (0,) * nd, pipeline_mode=pl.Buffered(1))


def _proj_jobs(u, w_ref, gb_ref, glu_store, q_ref, k_ref, v_ref, gate_ref, nc):
    def proj(off, c):
        return jnp.dot(u, w_ref[:, off + c:off + c + nc], preferred_element_type=F32)

    def glu(c):
        glu_store(c, proj(O_GLU, c) * _sigmoid(proj(O_GLU + C_CONV, c)))

    def heads(ref, off, scale, c):
        y = proj(off, c)
        if scale is not None:
            y = y * scale
        y = y.astype(BF16)
        for s in range(nc // LANES):
            ref[c // LANES + s] = y[:, s * LANES:(s + 1) * LANES]

    def gate(c):
        gate_ref[:, c:c + nc] = _sigmoid(proj(O_GATE, c) + gb_ref[:, c:c + nc]).astype(BF16)

    glu_jobs = [functools.partial(glu, c) for c in range(0, C_CONV, nc)]
    rest = []
    for ref, off, scale in ((q_ref, O_Q, LOG2_E * HEAD_DIM ** -0.5), (k_ref, O_K, None), (v_ref, O_V, None)):
        rest += [functools.partial(heads, ref, off, scale, c) for c in range(0, ATTN_W, nc)]
    rest += [functools.partial(gate, c) for c in range(0, 2 * D_MODEL, nc)]
    return glu_jobs, rest


def _meta_proj_kernel(x_ref, g_ref, w_ref, wb_ref, uc_ref, k_ref, v_ref, a_ref):
    j = pl.program_id(0)
    wb = w_ref[...].astype(BF16)
    wb_ref[...] = wb
    u = _rms_norm(x_ref[...], g_ref[...]).astype(BF16)
    y = jnp.dot(u, wb, preferred_element_type=F32)

    @pl.when(j == O_GLU // META_NC)
    def _():
        a_ref[...] = y

    @pl.when(j == (O_GLU + C_CONV) // META_NC)
    def _():
        uc_ref[...] = a_ref[...] * _sigmoid(y)

    for ref, off in ((k_ref, O_K), (v_ref, O_V)):
        @pl.when(j == off // META_NC)
        def _(ref=ref):
            yb = y.astype(BF16)
            for s in range(N_HEAD_PAIRS):
                ref[s] = yb[:, s * LANES:(s + 1) * LANES]


def _meta_proj(meta, g, w_f32):
    hp_shape = jax.ShapeDtypeStruct((N_HEAD_PAIRS, N_META, LANES), BF16)
    hp_spec = pl.BlockSpec((N_HEAD_PAIRS, N_META, LANES), lambda j: (0, 0, 0))
    return pl.pallas_call(
        _meta_proj_kernel,
        grid=(IN_W // META_NC,),
        in_specs=[_resident((N_META, D_MODEL)), _resident((1, D_MODEL)),
                  pl.BlockSpec((D_MODEL, META_NC), lambda j: (0, j))],
        out_specs=[pl.BlockSpec((D_MODEL, META_NC), lambda j: (0, j)),
                   pl.BlockSpec((N_META, C_CONV), lambda j: (0, 0)), hp_spec, hp_spec],
        out_shape=[jax.ShapeDtypeStruct((D_MODEL, IN_W), BF16),
                   jax.ShapeDtypeStruct((N_META, C_CONV), F32), hp_shape, hp_shape],
        scratch_shapes=[pltpu.VMEM((N_META, C_CONV), F32)],
        compiler_params=pltpu.CompilerParams(
            dimension_semantics=("arbitrary",), vmem_limit_bytes=48 * 1024 * 1024),
        name="meta_proj",
    )(meta, g, w_f32)


def _conv_block(win_ref, dww_ref, dwb_ref, y_ref, cb, nrows):
    base = CONV_HALO - (CONV_WIDTH - 1)
    cs = slice(cb * LANES, (cb + 1) * LANES)
    for rc in range(0, nrows, CONV_RC):
        total = None
        for phase in range(SUBLANES):
            acc = None
            for w in range(CONV_WIDTH):
                if (base + w) % SUBLANES != phase:
                    continue
                term = win_ref[rc + base + w:rc + base + w + CONV_RC, cs] * dww_ref[w:w + 1, cs]
                acc = term if acc is None else acc + term
            total = acc if total is None else total + acc
        y_ref[rc:rc + CONV_RC, cs] = total + dwb_ref[:, cs]


def _in_proj_kernel(tiles_per_batch, x_ref, g_ref, w_ref, gb_ref, ucm_ref, dww_ref, dwb_ref,
                    y_ref, q_ref, k_ref, v_ref, gate_ref, win_ref):
    tm = IN_TM
    i = pl.program_id(0)

    @pl.when(i % tiles_per_batch == 0)
    def _():
        win_ref[0:CONV_HALO - N_META, :] = jnp.zeros((CONV_HALO - N_META, C_CONV), F32)
        win_ref[CONV_HALO - N_META:CONV_HALO, :] = ucm_ref[...]

    @pl.when(i % tiles_per_batch != 0)
    def _():
        win_ref[0:CONV_HALO, :] = win_ref[tm:tm + CONV_HALO, :]

    u = _rms_norm(x_ref[...], g_ref[...]).astype(BF16)

    def glu_store(c, val):
        win_ref[CONV_HALO:, c:c + IN_NC] = val

    glu_jobs, rest = _proj_jobs(u, w_ref, gb_ref, glu_store, q_ref, k_ref, v_ref, gate_ref, IN_NC)
    for job in glu_jobs:
        job()
    conv_blocks = list(range(C_CONV // LANES))
    for job in rest:
        cb = conv_blocks.pop(0) if conv_blocks else None

        @pl.when(i >= 0)
        def _(job=job, cb=cb):
            job()
            if cb is not None:
                _conv_block(win_ref, dww_ref, dwb_ref, y_ref, cb, tm)
    assert not conv_blocks


def _in_proj(x2d, g, w_bf16, gate_b, uc_meta, dw_w, dw_b, seq):
    rows = x2d.shape[0]
    tm = IN_TM
    hp_spec = pl.BlockSpec((N_HEAD_PAIRS, tm, LANES), lambda i: (0, i, 0))
    hp_shape = jax.ShapeDtypeStruct((N_HEAD_PAIRS, rows, LANES), BF16)
    return pl.pallas_call(
        functools.partial(_in_proj_kernel, seq // tm),
        grid=(rows // tm,),
        in_specs=[
            pl.BlockSpec((tm, D_MODEL), lambda i: (i, 0)),
            _resident((1, D_MODEL)),
            _resident((D_MODEL, IN_W)),
            _resident((1, 2 * D_MODEL)),
            _resident((N_META, C_CONV)),
            _resident((CONV_WIDTH, C_CONV)),
            _resident((1, C_CONV)),
        ],
        out_specs=[
            pl.BlockSpec((tm, C_CONV), lambda i: (i, 0)),
            hp_spec, hp_spec, hp_spec,
            pl.BlockSpec((tm, 2 * D_MODEL), lambda i: (i, 0)),
        ],
        out_shape=[
            jax.ShapeDtypeStruct((rows, C_CONV), F32),
            hp_shape, hp_shape, hp_shape,
            jax.ShapeDtypeStruct((rows, 2 * D_MODEL), BF16),
        ],
        scratch_shapes=[pltpu.VMEM((CONV_HALO + tm, C_CONV), F32)],
        compiler_params=pltpu.CompilerParams(
            dimension_semantics=("arbitrary",), vmem_limit_bytes=48 * 1024 * 1024),
        name="in_proj",
    )(x2d, g, w_bf16, gate_b, uc_meta, dw_w, dw_b)


def _attn_kernel(q_ref, k_ref, v_ref, km_ref, vm_ref, o_ref, acc_ref, carry_ref):
    tq, tk = ATT_TQ, ATT_TK
    i = pl.program_id(2)
    chains = [(p, h) for p in range(ATT_NP) for h in range(2)]

    lane = lax.broadcasted_iota(jnp.int32, (tq, LANES), 1)
    in_head = (lane < HEAD_DIM, lane >= HEAD_DIM)
    qh = [jnp.where(in_head[h], q_ref[p], jnp.zeros((tq, LANES), BF16)) for p, h in chains]

    def tri(n):
        return (lax.broadcasted_iota(jnp.int32, (n, n), 0)
                >= lax.broadcasted_iota(jnp.int32, (n, n), 1)).astype(BF16)

    def softplus2(z, mask):
        sp = jnp.maximum(z, 0.0) + jnp.log2(1.0 + jnp.exp2(-jnp.abs(z)))
        if mask is not None:
            sp = jnp.where(mask, sp, 0.0)
        return sp.astype(BF16)

    def weights(z, r, mask):
        w = jnp.exp2(z - r)
        if mask is not None:
            w = jnp.where(mask, w, 0.0)
        return w.astype(BF16)

    def sweep(tiles, first=False):
        nc = len(chains)
        z = [[lax.dot_general(qh[c], ks[p], (((1,), (1,)), ((), ())), preferred_element_type=F32)
              for c, (p, h) in enumerate(chains)] for ks, _, _, _ in tiles]
        sp = [[softplus2(z[t][c], mask) for c in range(nc)]
              for t, (_, _, _, mask) in enumerate(tiles)]
        carry = None if first else [carry_ref[c] for c in range(nc)]
        r = []
        for t, (_, _, tri_kk, _) in enumerate(tiles):
            local = [jnp.dot(sp[t][c], tri_kk, preferred_element_type=F32) for c in range(nc)]
            r.append(local if carry is None else [local[c] + carry[c] for c in range(nc)])
            carry = [r[t][c][:, 0:1] for c in range(nc)]
        a = [[weights(z[t][c], r[t][c], mask) for c in range(nc)]
             for t, (_, _, _, mask) in enumerate(tiles)]
        for c, (p, h) in enumerate(chains):
            pv = None
            for t, (_, vs, _, _) in enumerate(tiles):
                d = jnp.dot(a[t][c], vs[p], preferred_element_type=F32)
                pv = d if pv is None else pv + d
            if first:
                acc_ref[c] = pv
            else:
                acc_ref[c] += pv
            carry_ref[c] = carry[c]

    def real_tile(j, mask):
        j0 = pl.multiple_of(j * tk, tk)
        return ([k_ref[p, pl.ds(j0, tk), :] for p in range(ATT_NP)],
                [v_ref[p, pl.ds(j0, tk), :] for p in range(ATT_NP)], tri(tk), mask)

    def meta_tile():
        colm = lax.broadcasted_iota(jnp.int32, (tq, META_PAD), 1)
        return ([km_ref[p] for p in range(ATT_NP)], [vm_ref[p] for p in range(ATT_NP)],
                tri(META_PAD), colm < N_META)

    def more_keys_matter():
        return (jnp.min(carry_ref[...]) < ATT_EXIT).astype(jnp.int32)

    diag_mask = (lax.broadcasted_iota(jnp.int32, (tq, tk), 1)
                 < lax.broadcasted_iota(jnp.int32, (tq, tk), 0))

    @pl.when(i == 0)
    def _():
        sweep([real_tile(i, diag_mask), meta_tile()], first=True)

    @pl.when(i > 0)
    def _():
        sweep([real_tile(i, diag_mask), real_tile(i - 1, None)], first=True)

        def body(state):
            n, _ = state
            sweep([real_tile(i - 1 - n, None)])
            return n + 1, more_keys_matter()

        _, go = lax.while_loop(lambda s: jnp.logical_and(s[0] < i, s[1] > 0), body,
                               (jnp.int32(1), more_keys_matter()))

        @pl.when(go > 0)
        def _():
            sweep([meta_tile()])

    for p in range(ATT_NP):
        o_ref[:, p * LANES:(p + 1) * LANES] = jnp.where(
            in_head[0], acc_ref[2 * p], acc_ref[2 * p + 1]).astype(o_ref.dtype)


def _attention(q, k, v, km, vm, batch, seq):
    nq = seq // ATT_TQ
    np_ = ATT_NP
    return pl.pallas_call(
        _attn_kernel,
        grid=(N_HEAD_PAIRS // np_, batch, nq),
        in_specs=[
            pl.BlockSpec((np_, ATT_TQ, LANES), lambda g, b, i: (g, b * nq + i, 0)),
            pl.BlockSpec((np_, seq, LANES), lambda g, b, i: (g, b, 0)),
            pl.BlockSpec((np_, seq, LANES), lambda g, b, i: (g, b, 0)),
            pl.BlockSpec((np_, META_PAD, LANES), lambda g, b, i: (g, 0, 0)),
            pl.BlockSpec((np_, META_PAD, LANES), lambda g, b, i: (g, 0, 0)),
        ],
        out_specs=pl.BlockSpec((ATT_TQ, np_ * LANES), lambda g, b, i: (b * nq + i, g)),
        out_shape=jax.ShapeDtypeStruct((batch * seq, ATTN_W), BF16),
        scratch_shapes=[
            pltpu.VMEM((2 * np_, ATT_TQ, LANES), F32),
            pltpu.VMEM((2 * np_, ATT_TQ, 1), F32),
        ],
        compiler_params=pltpu.CompilerParams(
            dimension_semantics=("arbitrary", "arbitrary", "arbitrary"),
            vmem_limit_bytes=58 * 1024 * 1024),
        name="sb_attn",
    )(q, k, v, km, vm)


def _post_kernel(y_ref, o_ref, gate_ref, x_ref, lng_ref, lnb_ref, wc_ref, wa_ref, wo_ref, pg_ref,
                 h1_ref):
    y_attn = jnp.dot(o_ref[...], wa_ref[...], preferred_element_type=F32)

    y = y_ref[...]
    mu = jnp.mean(y, axis=-1, keepdims=True)
    yc = y - mu
    var = jnp.mean(yc * yc, axis=-1, keepdims=True)
    n = yc * lax.rsqrt(var + LN_EPS) * lng_ref[...] + lnb_ref[...]
    c = (n * _sigmoid(n)).astype(BF16)

    y_conv = jnp.dot(c, wc_ref[...], preferred_element_type=F32)
    g_conv = gate_ref[:, :D_MODEL].astype(F32)
    g_attn = gate_ref[:, D_MODEL:].astype(F32)
    m = (g_conv * y_conv + g_attn * y_attn).astype(BF16)
    mix = jnp.dot(m, wo_ref[...], preferred_element_type=F32)
    h1_ref[...] = x_ref[...] + _rms_norm(mix, pg_ref[...])


def _post(y, o, gates, x2d, ln_g, ln_b, wc, wa, wo, pg):
    rows = x2d.shape[0]
    tm = POST_TM
    row_spec = lambda width: pl.BlockSpec((tm, width), lambda i: (i, 0))
    return pl.pallas_call(
        _post_kernel,
        grid=(rows // tm,),
        in_specs=[
            row_spec(C_CONV),
            row_spec(ATTN_W),
            row_spec(2 * D_MODEL),
            row_spec(D_MODEL),
            _resident((1, C_CONV)),
            _resident((1, C_CONV)),
            _resident((C_CONV, D_MODEL)),
            _resident((ATTN_W, D_MODEL)),
            _resident((D_MODEL, D_MODEL)),
            _resident((1, D_MODEL)),
        ],
        out_specs=row_spec(D_MODEL),
        out_shape=jax.ShapeDtypeStruct((rows, D_MODEL), F32),
        compiler_params=pltpu.CompilerParams(
            dimension_semantics=("arbitrary",), vmem_limit_bytes=40 * 1024 * 1024),
        name="post_mix",
    )(y, o, gates, x2d, ln_g, ln_b, wc, wa, wo, pg)


def _ffn_kernel(h_ref, g1_ref, wi_ref, wout_ref, g2_ref, out_ref, acc_ref):
    h = h_ref[...]
    u = _rms_norm(h, g1_ref[...]).astype(BF16)
    for n, c in enumerate(range(0, D_FF, FFN_NC)):
        wa = wi_ref[:, c:c + FFN_NC].astype(BF16)
        wb = wi_ref[:, D_FF + c:D_FF + c + FFN_NC].astype(BF16)
        a = jnp.dot(u, wa, preferred_element_type=F32)
        b = jnp.dot(u, wb, preferred_element_type=F32)
        s = (a * _sigmoid(a) * b).astype(BF16)
        part = jnp.dot(s, wout_ref[c:c + FFN_NC, :].astype(BF16), preferred_element_type=F32)
        if n == 0:
            acc_ref[...] = part
        else:
            acc_ref[...] += part
    out_ref[...] = h + _rms_norm(acc_ref[...], g2_ref[...])


def _ffn(h1, g1, wi, wout, g2):
    rows = h1.shape[0]
    tm = FFN_TM
    return pl.pallas_call(
        _ffn_kernel,
        grid=(rows // tm,),
        in_specs=[
            pl.BlockSpec((tm, D_MODEL), lambda i: (i, 0)),
            _resident((1, D_MODEL)),
            _resident((D_MODEL, 2 * D_FF)),
            _resident((D_FF, D_MODEL)),
            _resident((1, D_MODEL)),
        ],
        out_specs=pl.BlockSpec((tm, D_MODEL), lambda i: (i, 0)),
        out_shape=jax.ShapeDtypeStruct((rows, D_MODEL), F32),
        scratch_shapes=[pltpu.VMEM((tm, D_MODEL), F32)],
        compiler_params=pltpu.CompilerParams(
            dimension_semantics=("arbitrary",), vmem_limit_bytes=58 * 1024 * 1024),
        name="ffn",
    )(h1, g1, wi, wout, g2)


def kernel(x, meta_tokens, pre_mix_g, w_in, gate_b, dw_w, dw_b, conv_ln_g, conv_ln_b,
           w_conv_out, w_attn_out, w_o, post_mix_g, pre_ffn_g, w_ffn_in, w_ffn_out,
           post_ffn_g):
    batch, seq, d = x.shape
    assert d == D_MODEL and seq % ATT_TQ == 0 and seq % IN_TM == 0
    assert w_in.shape[0] == 1, "single layer"
    rows = batch * seq
    assert rows % POST_TM == 0 and rows % FFN_TM == 0

    row = lambda p: p[0].reshape(1, -1)
    x2d = x.reshape(rows, d)

    w_in_b, uc_m, k_m, v_m = _meta_proj(meta_tokens.astype(x.dtype), row(pre_mix_g), w_in[0])
    y, q, k, v, gates = _in_proj(x2d, row(pre_mix_g), w_in_b, row(gate_b), uc_m,
                                 dw_w[0], row(dw_b), seq)
    pad = ((0, 0), (0, META_PAD - N_META), (0, 0))
    o = _attention(q, k, v, jnp.pad(k_m, pad), jnp.pad(v_m, pad), batch, seq)

    h1 = _post(y, o, gates, x2d, row(conv_ln_g), row(conv_ln_b),
               w_conv_out[0].astype(BF16), w_attn_out[0].astype(BF16), w_o[0].astype(BF16),
               row(post_mix_g))
    out = _ffn(h1, row(pre_ffn_g), w_ffn_in[0], w_ffn_out[0], row(post_ffn_g))
    return out.reshape(batch, seq, d)
```
